```python
import jax, jax.numpy as jnp
from jax import lax
import numpy as np

D_MODEL = 1024
BATCH = 8
SEQ = 4096
DEPTH = 1

GRID_W = 64
N_HEADS = 8
N_KV_HEADS = 2
HEAD_DIM = 64
GQA_GROUP = N_HEADS // N_KV_HEADS
ATTN_WIDTH = N_HEADS * HEAD_DIM
KV_WIDTH = N_KV_HEADS * HEAD_DIM
LRU_WIDTH = D_MODEL - ATTN_WIDTH
LRU_BLOCKS = 8
LRU_BLOCK = LRU_WIDTH // LRU_BLOCKS
LRU_C = 8.0
LRU_CONV_W = 4
D_FF = 3 * D_MODEL
FFN_CONV_W = 3
Q_BLOCK = 128
ROPE_THETA = 10000.0
EPS = 1e-6
IN_WIDTH = ATTN_WIDTH + 2 * KV_WIDTH + 2 * LRU_WIDTH

kernel_name = "hymba_attn_rglru_convffn_encoder"


def rms_norm(x, g):
    xf = x.astype(jnp.float32)
    y = xf * lax.rsqrt(jnp.mean(xf * xf, axis=-1, keepdims=True) + EPS)
    return (y * g.astype(jnp.float32)).astype(x.dtype)


def depthwise_conv(x, w, b, left, right):
    y = lax.conv_general_dilated(
        x, w[:, None, :].astype(x.dtype), window_strides=(1,), padding=[(left, right)],
        dimension_numbers=("NWC", "WIO", "NWC"), feature_group_count=x.shape[-1])
    return y + b.astype(x.dtype)


def rope_1d(x, pos):
    d = x.shape[-1]
    inv_freq = 1.0 / (ROPE_THETA ** (jnp.arange(0, d, 2, dtype=jnp.float32) / d))
    ang = pos.astype(jnp.float32)[:, None] * inv_freq[None, :]
    cos = jnp.cos(ang)[None, :, None, :]
    sin = jnp.sin(ang)[None, :, None, :]
    xf = x.astype(jnp.float32)
    x1, x2 = xf[..., : d // 2], xf[..., d // 2:]
    return jnp.concatenate([x1 * cos - x2 * sin, x2 * cos + x1 * sin], axis=-1).astype(x.dtype)


def axial_rope(x, row, col):
    half = x.shape[-1] // 2
    return jnp.concatenate([rope_1d(x[..., :half], row), rope_1d(x[..., half:], col)], axis=-1)


def block_attention(q, k, v):
    B, S = q.shape[0], q.shape[1]
    nb = S // Q_BLOCK
    qg = q.reshape(B, nb, Q_BLOCK, N_KV_HEADS, GQA_GROUP, HEAD_DIM).transpose(1, 0, 2, 3, 4, 5)

    def one_block(qb):
        s = jnp.einsum("bqkgd,bskd->bkgqs", qb, k).astype(jnp.float32)
        p = jax.nn.softmax(s, axis=-1).astype(v.dtype)
        return jnp.einsum("bkgqs,bskd->bqkgd", p, v)

    o = lax.map(one_block, qg)
    return o.transpose(1, 0, 2, 3, 4, 5).reshape(B, S, ATTN_WIDTH)


def _linear_combine(left, right):
    a1, b1 = left
    a2, b2 = right
    return a1 * a2, a2 * b1 + b2


def rg_lru(xc, wa, ba, wx, bx, lam, reverse):
    B, S, _ = xc.shape
    xf = xc.astype(jnp.float32)
    xb = xf.reshape(B, S, LRU_BLOCKS, LRU_BLOCK)
    r = jax.nn.sigmoid(jnp.einsum("bshi,hij->bshj", xb, wa.astype(jnp.float32)) + ba.astype(jnp.float32))
    i = jax.nn.sigmoid(jnp.einsum("bshi,hij->bshj", xb, wx.astype(jnp.float32)) + bx.astype(jnp.float32))
    r = r.reshape(B, S, LRU_WIDTH)
    i = i.reshape(B, S, LRU_WIDTH)
    log_a = LRU_C * r * jax.nn.log_sigmoid(lam.astype(jnp.float32))
    a = jnp.exp(log_a)
    mult = jnp.sqrt(-jnp.expm1(2.0 * log_a))
    start = S - 1 if reverse else 0
    mult = jnp.where((jnp.arange(S) == start)[None, :, None], 1.0, mult)
    b = mult * i * xf
    _, h = lax.associative_scan(_linear_combine, (a, b), reverse=reverse, axis=1)
    return h


def setup_inputs(seed: int = 0) -> dict:
    key = jax.random.key(seed)
    ks = jax.random.split(key, 32)
    f32 = jnp.float32

    def nrm(k, shape, scale):
        return jax.random.normal(k, shape, f32) * scale

    def gain(k, shape):
        return 1.0 + 0.02 * jax.random.normal(k, shape, f32)

    def lam_init(k):
        u = jax.random.uniform(k, (DEPTH, LRU_WIDTH), f32, 0.9, 0.999)
        a = u ** (1.0 / LRU_C)
        return jnp.log(a) - jnp.log1p(-a)

    L = DEPTH
    return {
        "x": jax.random.normal(ks[0], (BATCH, SEQ, D_MODEL), f32),
        "norm1_g": gain(ks[1], (L, D_MODEL)),
        "w_in": nrm(ks[2], (L, D_MODEL, IN_WIDTH), D_MODEL ** -0.5),
        "q_norm_g": gain(ks[3], (L, HEAD_DIM)),
        "k_norm_g": gain(ks[4], (L, HEAD_DIM)),
        "lru_conv_w": nrm(ks[5], (L, LRU_CONV_W, LRU_WIDTH), LRU_CONV_W ** -0.5),
        "lru_conv_b": nrm(ks[6], (L, LRU_WIDTH), 0.01),
        "wa_f": nrm(ks[7], (L, LRU_BLOCKS, LRU_BLOCK, LRU_BLOCK), LRU_BLOCK ** -0.5),
        "ba_f": nrm(ks[8], (L, LRU_BLOCKS, LRU_BLOCK), 0.01),
        "wx_f": nrm(ks[9], (L, LRU_BLOCKS, LRU_BLOCK, LRU_BLOCK), LRU_BLOCK ** -0.5),
        "bx_f": nrm(ks[10], (L, LRU_BLOCKS, LRU_BLOCK), 0.01),
        "lam_f": lam_init(ks[11]),
        "wa_b": nrm(ks[12], (L, LRU_BLOCKS, LRU_BLOCK, LRU_BLOCK), LRU_BLOCK ** -0.5),
        "ba_b": nrm(ks[13], (L, LRU_BLOCKS, LRU_BLOCK), 0.01),
        "wx_b": nrm(ks[14], (L, LRU_BLOCKS, LRU_BLOCK, LRU_BLOCK), LRU_BLOCK ** -0.5),
        "bx_b": nrm(ks[15], (L, LRU_BLOCKS, LRU_BLOCK), 0.01),
        "lam_b": lam_init(ks[16]),
        "w_out": nrm(ks[17], (L, D_MODEL, D_MODEL), D_MODEL ** -0.5),
        "norm2_g": gain(ks[18], (L, D_MODEL)),
        "w_up": nrm(ks[19], (L, D_MODEL, 2 * D_FF), D_MODEL ** -0.5),
        "up_conv_w": nrm(ks[20], (L, FFN_CONV_W, 2 * D_FF), FFN_CONV_W ** -0.5),
        "up_conv_b": nrm(ks[21], (L, 2 * D_FF), 0.01),
        "w_down": nrm(ks[22], (L, D_FF, D_MODEL), D_FF ** -0.5),
        "final_g": gain(ks[23], (D_MODEL,)),
    }


def reference(x, norm1_g, w_in, q_norm_g, k_norm_g, lru_conv_w, lru_conv_b,
              wa_f, ba_f, wx_f, bx_f, lam_f, wa_b, ba_b, wx_b, bx_b, lam_b,
              w_out, norm2_g, w_up, up_conv_w, up_conv_b, w_down, final_g):
    B, S, _ = x.shape
    rows = S // GRID_W
    row = jnp.repeat(jnp.arange(rows, dtype=jnp.int32), GRID_W)
    col = jnp.tile(jnp.arange(GRID_W, dtype=jnp.int32), rows)
    splits = [ATTN_WIDTH, ATTN_WIDTH + KV_WIDTH, ATTN_WIDTH + 2 * KV_WIDTH,
              ATTN_WIDTH + 2 * KV_WIDTH + LRU_WIDTH]

    for l in range(DEPTH):
        h = rms_norm(x, norm1_g[l])
        z = h @ w_in[l]
        q, k, v, xr, yr = jnp.split(z, splits, axis=-1)

        q = rms_norm(q.reshape(B, S, N_HEADS, HEAD_DIM), q_norm_g[l])
        k = rms_norm(k.reshape(B, S, N_KV_HEADS, HEAD_DIM), k_norm_g[l])
        q = axial_rope(q, row, col) * (HEAD_DIM ** -0.5)
        k = axial_rope(k, row, col)
        v = v.reshape(B, S, N_KV_HEADS, HEAD_DIM)
        attn_out = block_attention(q, k, v)

        xc = depthwise_conv(xr, lru_conv_w[l], lru_conv_b[l], 2, 1)
        h_fwd = rg_lru(xc, wa_f[l], ba_f[l], wx_f[l], bx_f[l], lam_f[l], False)
        h_bwd = rg_lru(xc, wa_b[l], ba_b[l], wx_b[l], bx_b[l], lam_b[l], True)
        lru_out = ((h_fwd + h_bwd) * jax.nn.gelu(yr.astype(jnp.float32))).astype(x.dtype)

        mixed = jnp.concatenate([attn_out, lru_out], axis=-1) @ w_out[l]
        x = x + mixed

        h = rms_norm(x, norm2_g[l])
        u = depthwise_conv(h @ w_up[l], up_conv_w[l], up_conv_b[l], 1, 1)
        gate, val = jnp.split(u, 2, axis=-1)
        x = x + (jax.nn.gelu(gate) * val) @ w_down[l]

    return rms_norm(x, final_g)
```

```python
import functools
import math

import jax
import jax.numpy as jnp
from jax import lax
from jax.experimental import pallas as pl
from jax.experimental.pallas import tpu as pltpu

F32 = jnp.float32
BF16 = jnp.bfloat16

N_HEADS = 8
N_KV_HEADS = 2
HEAD_DIM = 64
GQA_GROUP = N_HEADS // N_KV_HEADS
ATTN_WIDTH = N_HEADS * HEAD_DIM
KV_WIDTH = N_KV_HEADS * HEAD_DIM
LRU_BLOCK = 64
LRU_C = 8.0
GRID_W = 64
ROPE_THETA = 10000.0
EPS = 1e-6

LANES = 128
SUBLANES = 8
BF16_ROWS = 16
MXU_DIM = 256
VMEM_LIMIT_BYTES = 56 * 1024 * 1024

PROJ_TM = 512
ATTN_TQ = 256
ATTN_KC = 256
DEN_ROWS = BF16_ROWS
LRU_TC = 256
FFN_TM = 1024
FFN_FC = 512
FFN_HALO = BF16_ROWS


def _gelu_tanh(x):
    return 0.5 * x * (1.0 + jnp.tanh(math.sqrt(2.0 / math.pi) * (x + 0.044715 * (x * x * x))))


def _sigmoid(x):
    return 0.5 * jnp.tanh(0.5 * x) + 0.5


def _in_proj_kernel(x_ref, g1_ref, w_ref, gqk_ref, cos_ref, sa_ref, sb_ref, e_ref,
                    q_ref, k_ref, vt_ref, xr_ref, gy_ref):
    x = x_ref[0]
    ms = jnp.mean(x * x, axis=-1, keepdims=True)
    h = (x * lax.rsqrt(ms + EPS) * g1_ref[...]).astype(BF16)

    n_qkv = ATTN_WIDTH + 2 * KV_WIDTH
    z = jnp.dot(h, w_ref[:, 0:n_qkv], preferred_element_type=F32)

    sq = (z * z).astype(BF16)
    e = e_ref[...]
    sums = [jnp.dot(sq[:, c * MXU_DIM:(c + 1) * MXU_DIM], e, preferred_element_type=F32)
            for c in range(n_qkv // MXU_DIM)]
    ssum = jnp.concatenate(sums, axis=1)
    n_qk = ATTN_WIDTH + KV_WIDTH
    qk = z[:, 0:n_qk] * lax.rsqrt(ssum[:, 0:n_qk] * (1.0 / HEAD_DIM) + EPS) * gqk_ref[...]

    cos = cos_ref[...]
    sa = sa_ref[...]
    sb = sb_ref[...]
    roped = []
    for c in range(n_qk // LANES):
        xc = qk[:, c * LANES:(c + 1) * LANES]
        up = pltpu.roll(xc, LANES - 16, 1)
        dn = pltpu.roll(xc, 16, 1)
        roped.append(xc * cos + up * sa + dn * sb)
    for c in range(ATTN_WIDTH // LANES):
        q_ref[0, :, c * LANES:(c + 1) * LANES] = roped[c].astype(BF16)
    k_ref[0] = roped[ATTN_WIDTH // LANES].astype(BF16)

    vt = z[:, n_qk:n_qkv].T.astype(BF16)
    for c in range(vt_ref.shape[1]):
        vt_ref[0, c] = vt[:, c * ATTN_KC:(c + 1) * ATTN_KC]

    lru_w = xr_ref.shape[2]
    xr_ref[0] = jnp.dot(h, w_ref[:, n_qkv:n_qkv + lru_w], preferred_element_type=F32)
    yr = jnp.dot(h, w_ref[:, n_qkv + lru_w:n_qkv + 2 * lru_w], preferred_element_type=F32)
    gy_ref[0] = _gelu_tanh(yr).astype(BF16)


def _in_proj(x, g1, w_in, gqk, cos, sa, sb, e):
    B, S, D = x.shape
    tm = PROJ_TM
    n_in = w_in.shape[1]
    lru_w = (n_in - ATTN_WIDTH - 2 * KV_WIDTH) // 2
    n_qk = ATTN_WIDTH + KV_WIDTH
    grid = (B, S // tm)
    tab = pl.BlockSpec((tm, LANES), lambda b, i: (i, 0))
    return pl.pallas_call(
        _in_proj_kernel,
        grid=grid,
        in_specs=[
            pl.BlockSpec((1, tm, D), lambda b, i: (b, i, 0)),
            pl.BlockSpec((1, D), lambda b, i: (0, 0)),
            pl.BlockSpec((D, n_in), lambda b, i: (0, 0)),
            pl.BlockSpec((1, n_qk), lambda b, i: (0, 0)),
            tab, tab, tab,
            pl.BlockSpec((MXU_DIM, MXU_DIM), lambda b, i: (0, 0)),
        ],
        out_specs=[
            pl.BlockSpec((1, tm, ATTN_WIDTH), lambda b, i: (b, i, 0)),
            pl.BlockSpec((1, tm, KV_WIDTH), lambda b, i: (b, i, 0)),
            pl.BlockSpec((1, tm // ATTN_KC, KV_WIDTH, ATTN_KC), lambda b, i: (b, i, 0, 0)),
            pl.BlockSpec((1, tm, lru_w), lambda b, i: (b, i, 0)),
            pl.BlockSpec((1, tm, lru_w), lambda b, i: (b, i, 0)),
        ],
        out_shape=[
            jax.ShapeDtypeStruct((B, S, ATTN_WIDTH), BF16),
            jax.ShapeDtypeStruct((B, S, KV_WIDTH), BF16),
            jax.ShapeDtypeStruct((B, S // ATTN_KC, KV_WIDTH, ATTN_KC), BF16),
            jax.ShapeDtypeStruct((B, S, lru_w), F32),
            jax.ShapeDtypeStruct((B, S, lru_w), BF16),
        ],
        compiler_params=pltpu.CompilerParams(
            dimension_semantics=("arbitrary", "arbitrary"),
            vmem_limit_bytes=VMEM_LIMIT_BYTES),
        name="in_proj",
    )(x, g1, w_in, gqk, cos, sa, sb, e)


def _attn_kernel(q_ref, k_ref, vt_ref, o_ref):
    tq = q_ref.shape[1]
    n_chunks = k_ref.shape[1] // ATTN_KC
    ones = jnp.ones((DEN_ROWS, ATTN_KC), BF16)
    zeros = jnp.zeros((tq, HEAD_DIM), BF16)
    for h in range(N_HEADS):
        j = h // GQA_GROUP
        qh = q_ref[0, :, h * HEAD_DIM:(h + 1) * HEAD_DIM]
        qpad = jnp.concatenate([qh, zeros] if j == 0 else [zeros, qh], axis=1)

        def chunk(c, carry, j=j, qpad=qpad):
            m, acc = carry
            start = pl.multiple_of(c * ATTN_KC, ATTN_KC)
            kc = k_ref[0, pl.ds(start, ATTN_KC), :]
            s = lax.dot_general(kc, qpad, (((1,), (1,)), ((), ())),
                                preferred_element_type=F32)
            m_new = jnp.maximum(m, jnp.max(s, axis=0, keepdims=True))
            alpha = jnp.exp(m - m_new)
            p = jnp.exp(s - m_new).astype(BF16)
            vext = jnp.concatenate(
                [vt_ref[0, c, j * HEAD_DIM:(j + 1) * HEAD_DIM, :], ones], axis=0)
            pv = jnp.dot(vext, p, preferred_element_type=F32)
            return m_new, acc * alpha + pv

        m0 = jnp.full((1, tq), -jnp.inf, F32)
        acc0 = jnp.zeros((HEAD_DIM + DEN_ROWS, tq), F32)
        _, acc = lax.fori_loop(0, n_chunks, chunk, (m0, acc0))
        o = acc[0:HEAD_DIM] / acc[HEAD_DIM:HEAD_DIM + 1]
        o_ref[0, h * HEAD_DIM:(h + 1) * HEAD_DIM, :] = o.astype(BF16)


def _attention(q, k, vt):
    B, S, _ = q.shape
    tq = ATTN_TQ
    return pl.pallas_call(
        _attn_kernel,
        grid=(B, S // tq),
        in_specs=[
            pl.BlockSpec((1, tq, ATTN_WIDTH), lambda b, i: (b, i, 0)),
            pl.BlockSpec((1, S, KV_WIDTH), lambda b, i: (b, 0, 0)),
            pl.BlockSpec((1, S // ATTN_KC, KV_WIDTH, ATTN_KC), lambda b, i: (b, 0, 0, 0)),
        ],
        out_specs=pl.BlockSpec((1, ATTN_WIDTH, tq), lambda b, i: (b, 0, i)),
        out_shape=jax.ShapeDtypeStruct((B, ATTN_WIDTH, S), BF16),
        compiler_params=pltpu.CompilerParams(
            dimension_semantics=("arbitrary", "arbitrary"),
            vmem_limit_bytes=VMEM_LIMIT_BYTES),
        name="attention",
    )(q, k, vt)


def _scan8(a, b, row, reverse):
    for sh in (1, 2, 4):
        if reverse:
            keep = row < (SUBLANES - sh)
            a_s = pltpu.roll(a, SUBLANES - sh, 0)
            b_s = pltpu.roll(b, SUBLANES - sh, 0)
        else:
            keep = row >= sh
            a_s = pltpu.roll(a, sh, 0)
            b_s = pltpu.roll(b, sh, 0)
        b = jnp.where(keep, a * b_s, 0.0) + b
        a = jnp.where(keep, a * a_s, a)
    return a, b


def _lru_kernel(x_ref, gy_ref, cw_ref, cb_ref, wg_ref, bg_ref, lamf_ref, lamb_ref,
                o_ref, xpad, af, bf, ab, bb):
    S = x_ref.shape[1]
    W = x_ref.shape[2]
    tc = LRU_TC
    pad = SUBLANES

    xpad[0:pad, :] = jnp.zeros((pad, W), F32)
    xpad[pad + S:pad + S + pad, :] = jnp.zeros((pad, W), F32)
    xpad[pad:pad + S, :] = x_ref[0]

    def log_sigmoid(v):
        return -(jnp.maximum(-v, 0.0) + jnp.log1p(jnp.exp(-jnp.abs(v))))

    ls_f = LRU_C * log_sigmoid(lamf_ref[...])
    ls_b = LRU_C * log_sigmoid(lamb_ref[...])
    cw = cw_ref[...]
    cb = cb_ref[...]
    bg = bg_ref[0]
    trow = lax.broadcasted_iota(jnp.int32, (tc, W), 0)

    def gates(ci, _):
        t0 = pl.multiple_of(ci * tc, tc)
        xw = xpad[pl.ds(t0, tc + 2 * pad), :]
        xc = (cw[0:1] * xw[pad - 2:pad - 2 + tc] + cw[1:2] * xw[pad - 1:pad - 1 + tc]
              + cw[2:3] * xw[pad:pad + tc] + cw[3:4] * xw[pad + 1:pad + 1 + tc] + cb)
        g = jnp.dot(xc.astype(BF16), wg_ref[0], preferred_element_type=F32) + bg
        t = trow + t0
        for d, (ls, a_s, b_s, first) in enumerate(((ls_f, af, bf, 0), (ls_b, ab, bb, S - 1))):
            r = _sigmoid(g[:, (2 * d) * W:(2 * d + 1) * W])
            i = _sigmoid(g[:, (2 * d + 1) * W:(2 * d + 2) * W])
            log_a = r * ls
            a = jnp.exp(log_a)
            mult = jnp.sqrt(-jnp.tanh(log_a) * (1.0 + a * a))
            mult = jnp.where(t == first, 1.0, mult)
            a_s[pl.ds(t0, tc), :] = a
            b_s[pl.ds(t0, tc), :] = mult * i * xc
        return 0

    lax.fori_loop(0, S // tc, gates, 0)

    row = lax.broadcasted_iota(jnp.int32, (SUBLANES, W), 0)
    n_rows = S // SUBLANES

    def scan(i, carry):
        hf, hb = carry
        tf = pl.multiple_of(i * SUBLANES, SUBLANES)
        tb = pl.multiple_of((n_rows - 1 - i) * SUBLANES, SUBLANES)
        A, Bv = _scan8(af[pl.ds(tf, SUBLANES), :], bf[pl.ds(tf, SUBLANES), :], row, False)
        h = Bv + A * hf
        bf[pl.ds(tf, SUBLANES), :] = h
        hf = jnp.broadcast_to(h[SUBLANES - 1:SUBLANES, :], (SUBLANES, W))
        A, Bv = _scan8(ab[pl.ds(tb, SUBLANES), :], bb[pl.ds(tb, SUBLANES), :], row, True)
        h = Bv + A * hb
        bb[pl.ds(tb, SUBLANES), :] = h
        hb = jnp.broadcast_to(h[0:1, :], (SUBLANES, W))
        return hf, hb

    z8 = jnp.zeros((SUBLANES, W), F32)
    lax.fori_loop(0, n_rows, scan, (z8, z8), unroll=4)

    def combine(ci, _):
        t0 = pl.multiple_of(ci * tc, tc)
        hs = bf[pl.ds(t0, tc), :] + bb[pl.ds(t0, tc), :]
        o_ref[0, pl.ds(t0, tc), :] = (hs * gy_ref[0, pl.ds(t0, tc), :].astype(F32)).astype(BF16)
        return 0

    lax.fori_loop(0, S // tc, combine, 0)


def _lru(xr, gy, conv_w, conv_b, wg, bg, lam_f, lam_b):
    B, S, C = xr.shape
    W = LANES
    n_grp = C // W
    seq = pl.BlockSpec((1, S, W), lambda b, c: (b, 0, c))
    vec = pl.BlockSpec((1, W), lambda b, c: (0, c))
    return pl.pallas_call(
        _lru_kernel,
        grid=(B, n_grp),
        in_specs=[
            seq, seq,
            pl.BlockSpec((conv_w.shape[0], W), lambda b, c: (0, c)),
            vec,
            pl.BlockSpec((1, W, 4 * W), lambda b, c: (c, 0, 0)),
            pl.BlockSpec((1, 1, 4 * W), lambda b, c: (c, 0, 0)),
            vec, vec,
        ],
        out_specs=seq,
        out_shape=jax.ShapeDtypeStruct((B, S, C), BF16),
        scratch_shapes=[
            pltpu.VMEM((S + 2 * SUBLANES, W), F32),
            pltpu.VMEM((S, W), F32), pltpu.VMEM((S, W), F32),
            pltpu.VMEM((S, W), F32), pltpu.VMEM((S, W), F32),
        ],
        compiler_params=pltpu.CompilerParams(
            dimension_semantics=("arbitrary", "arbitrary"),
            vmem_limit_bytes=VMEM_LIMIT_BYTES),
        name="rg_lru",
    )(xr, gy, conv_w, conv_b, wg, bg, lam_f, lam_b)


def _out_proj_kernel(ot_ref, lru_ref, x_ref, wa_ref, wl_ref, g2_ref, x1_ref, h2_ref):
    mixed = lax.dot_general(ot_ref[0], wa_ref[...], (((0,), (0,)), ((), ())),
                            preferred_element_type=F32)
    mixed = mixed + jnp.dot(lru_ref[0], wl_ref[...], preferred_element_type=F32)
    x1 = x_ref[0] + mixed
    x1_ref[0] = x1
    ms = jnp.mean(x1 * x1, axis=-1, keepdims=True)
    h2_ref[0] = (x1 * lax.rsqrt(ms + EPS) * g2_ref[...]).astype(BF16)


def _out_proj(ot, lru, x, wo_a, wo_l, g2):
    B, S, D = x.shape
    tm = PROJ_TM
    return pl.pallas_call(
        _out_proj_kernel,
        grid=(B, S // tm),
        in_specs=[
            pl.BlockSpec((1, ATTN_WIDTH, tm), lambda b, i: (b, 0, i)),
            pl.BlockSpec((1, tm, lru.shape[2]), lambda b, i: (b, i, 0)),
            pl.BlockSpec((1, tm, D), lambda b, i: (b, i, 0)),
            pl.BlockSpec(wo_a.shape, lambda b, i: (0, 0)),
            pl.BlockSpec(wo_l.shape, lambda b, i: (0, 0)),
            pl.BlockSpec((1, D), lambda b, i: (0, 0)),
        ],
        out_specs=[
            pl.BlockSpec((1, tm, D), lambda b, i: (b, i, 0)),
            pl.BlockSpec((1, tm, D), lambda b, i: (b, i, 0)),
        ],
        out_shape=[
            jax.ShapeDtypeStruct((B, S, D), F32),
            jax.ShapeDtypeStruct((B, S, D), BF16),
        ],
        compiler_params=pltpu.CompilerParams(
            dimension_semantics=("arbitrary", "arbitrary"),
            vmem_limit_bytes=VMEM_LIMIT_BYTES),
        name="out_proj",
    )(ot, lru, x, wo_a, wo_l, g2)


def _ffn_kernel(hp_ref, h_ref, hn_ref, x1_ref, wg_ref, wv_ref, cwg_ref, cwv_ref, cbg_ref, cbv_ref,
                wd_ref, gf_ref, o_ref, acc_ref):
    i = pl.program_id(1)
    j = pl.program_id(2)
    tm = h_ref.shape[1]
    halo = FFN_HALO
    hcat = jnp.concatenate([hp_ref[0], h_ref[0], hn_ref[0]], axis=0)
    row = lax.broadcasted_iota(jnp.int32, (tm, 1), 0)
    first = row == jnp.where(i == 0, 0, -1)
    last = row == jnp.where(i == pl.num_programs(1) - 1, tm - 1, -1)

    def conv(w_ref, cw_ref, cb_ref):
        u = jnp.dot(hcat, w_ref[...], preferred_element_type=F32)
        cw = cw_ref[...]
        prev = jnp.where(first, 0.0, u[halo - 1:halo - 1 + tm])
        nxt = jnp.where(last, 0.0, u[halo + 1:halo + 1 + tm])
        return cw[0:1] * prev + cw[1:2] * u[halo:halo + tm] + cw[2:3] * nxt + cb_ref[...]

    gate = conv(wg_ref, cwg_ref, cbg_ref)
    val = conv(wv_ref, cwv_ref, cbv_ref)
    act = (_gelu_tanh(gate) * val).astype(BF16)
    part = jnp.dot(act, wd_ref[...], preferred_element_type=F32)

    @pl.when(j == 0)
    def _():
        acc_ref[...] = part

    @pl.when(j > 0)
    def _():
        acc_ref[...] += part

    @pl.when(j == pl.num_programs(2) - 1)
    def _():
        x2 = x1_ref[0] + acc_ref[...]
        ms = jnp.mean(x2 * x2, axis=-1, keepdims=True)
        o_ref[0] = x2 * lax.rsqrt(ms + EPS) * gf_ref[...]


def _ffn(h2, x1, w_up, cw, cb, w_down, gf):
    B, S, D = x1.shape
    tm, fc, halo = FFN_TM, FFN_FC, FFN_HALO
    d_ff = w_down.shape[0]
    n_ff = d_ff // fc
    hb = tm // halo
    n_hb = S // halo
    return pl.pallas_call(
        _ffn_kernel,
        grid=(B, S // tm, n_ff),
        in_specs=[
            pl.BlockSpec((1, halo, D), lambda b, i, j: (b, jnp.maximum(i * hb - 1, 0), 0)),
            pl.BlockSpec((1, tm, D), lambda b, i, j: (b, i, 0)),
            pl.BlockSpec((1, halo, D), lambda b, i, j: (b, jnp.minimum((i + 1) * hb, n_hb - 1), 0)),
            pl.BlockSpec((1, tm, D), lambda b, i, j: (b, i, 0)),
            pl.BlockSpec((D, fc), lambda b, i, j: (0, j)),
            pl.BlockSpec((D, fc), lambda b, i, j: (0, n_ff + j)),
            pl.BlockSpec((cw.shape[0], fc), lambda b, i, j: (0, j)),
            pl.BlockSpec((cw.shape[0], fc), lambda b, i, j: (0, n_ff + j)),
            pl.BlockSpec((1, fc), lambda b, i, j: (0, j)),
            pl.BlockSpec((1, fc), lambda b, i, j: (0, n_ff + j)),
            pl.BlockSpec((fc, D), lambda b, i, j: (j, 0)),
            pl.BlockSpec((1, D), lambda b, i, j: (0, 0)),
        ],
        out_specs=pl.BlockSpec((1, tm, D), lambda b, i, j: (b, i, 0)),
        out_shape=jax.ShapeDtypeStruct((B, S, D), F32),
        scratch_shapes=[pltpu.VMEM((tm, D), F32)],
        compiler_params=pltpu.CompilerParams(
            dimension_semantics=("arbitrary", "arbitrary", "arbitrary"),
            vmem_limit_bytes=VMEM_LIMIT_BYTES),
        name="conv_ffn",
    )(h2, h2, h2, x1, w_up, w_up, cw, cw, cb, cb, w_down, gf)


def _rope_tables(S):
    t = jnp.arange(S, dtype=jnp.int32)
    pos = jnp.stack([t // GRID_W, t % GRID_W], axis=1).astype(F32)
    quarter = HEAD_DIM // 4
    inv_freq = 1.0 / (ROPE_THETA ** (jnp.arange(0, 2 * quarter, 2, dtype=F32) / (2 * quarter)))
    ang = pos[:, :, None] * inv_freq[None, None, :]
    cos = jnp.cos(ang)
    sin = jnp.sin(ang)
    zero = jnp.zeros_like(sin)
    cos_h = jnp.concatenate([cos, cos], axis=2).reshape(S, HEAD_DIM)
    sa_h = jnp.concatenate([-sin, zero], axis=2).reshape(S, HEAD_DIM)
    sb_h = jnp.concatenate([zero, sin], axis=2).reshape(S, HEAD_DIM)
    rep = LANES // HEAD_DIM
    return (jnp.tile(cos_h, (1, rep)), jnp.tile(sa_h, (1, rep)), jnp.tile(sb_h, (1, rep)))


def _block_diag_pairs(w):
    nb, bs, _ = w.shape
    w2 = w.reshape(nb // 2, 2, bs, bs)
    z = jnp.zeros((nb // 2, bs, bs), w.dtype)
    top = jnp.concatenate([w2[:, 0], z], axis=2)
    bot = jnp.concatenate([z, w2[:, 1]], axis=2)
    return jnp.concatenate([top, bot], axis=1)


def kernel(x, norm1_g, w_in, q_norm_g, k_norm_g, lru_conv_w, lru_conv_b, wa_f, ba_f, wx_f, bx_f, lam_f,
           wa_b, ba_b, wx_b, bx_b, lam_b, w_out, norm2_g, w_up, up_conv_w, up_conv_b, w_down, final_g):
    B, S, D = x.shape
    depth = w_in.shape[0]
    assert depth == 1, "the final RMSNorm is fused into the (single) layer's channel mixer"
    lru_w = lam_f.shape[1]
    n_grp = lru_w // LANES
    cos, sa, sb = _rope_tables(S)
    idx = jnp.arange(MXU_DIM) // HEAD_DIM
    e = (idx[:, None] == idx[None, :]).astype(BF16)

    for l in range(depth):
        gqk = jnp.concatenate([jnp.tile(q_norm_g[l] * (HEAD_DIM ** -0.5), N_HEADS),
                               jnp.tile(k_norm_g[l], N_KV_HEADS)])[None, :]
        q, k, vt, xr, gy = _in_proj(x, norm1_g[l][None, :], w_in[l].astype(BF16), gqk, cos, sa, sb, e)
        ot = _attention(q, k, vt)

        wg = jnp.concatenate([_block_diag_pairs(w[l]) for w in (wa_f, wx_f, wa_b, wx_b)],
                             axis=2).astype(BF16)
        bg = jnp.concatenate([b[l].reshape(n_grp, 1, LANES) for b in (ba_f, bx_f, ba_b, bx_b)],
                             axis=2)
        lru = _lru(xr, gy, lru_conv_w[l], lru_conv_b[l][None, :], wg, bg,
                   lam_f[l][None, :], lam_b[l][None, :])

        wo = w_out[l].astype(BF16)
        x1, h2 = _out_proj(ot, lru, x, wo[:ATTN_WIDTH], wo[ATTN_WIDTH:], norm2_g[l][None, :])
        x = _ffn(h2, x1, w_up[l].astype(BF16), up_conv_w[l], up_conv_b[l][None, :],
                 w_down[l].astype(BF16), final_g[None, :])
    return x
```

```python
import functools
import math

import jax
import jax.numpy as jnp
from jax import lax
from jax.experimental import pallas as pl
from jax.experimental.pallas import tpu as pltpu

F32 = jnp.float32
BF16 = jnp.bfloat16

N_HEADS = 8
N_KV_HEADS = 2
HEAD_DIM = 64
GQA_GROUP = N_HEADS // N_KV_HEADS
ATTN_WIDTH = N_HEADS * HEAD_DIM
KV_WIDTH = N_KV_HEADS * HEAD_DIM
LRU_BLOCK = 64
LRU_C = 8.0
GRID_W = 64
ROPE_THETA = 10000.0
EPS = 1e-6

LANES = 128
SUBLANES = 8
BF16_ROWS = 16
MXU_DIM = 256
VMEM_LIMIT_BYTES = 56 * 1024 * 1024

PROJ_TM = 512
ATTN_TQ = 256
DEN_ROWS = BF16_ROWS
LRU_TC = 256
FFN_TM = 1024
FFN_FC = 512
FFN_HALO = BF16_ROWS


def _gelu_tanh(x):
    return 0.5 * x * (1.0 + jnp.tanh(math.sqrt(2.0 / math.pi) * (x + 0.044715 * (x * x * x))))


def _sigmoid(x):
    return 0.5 * jnp.tanh(0.5 * x) + 0.5


def _in_proj_kernel(x_ref, g1_ref, w_ref, gqk_ref, cos_ref, sa_ref, sb_ref, e_ref,
                    q_ref, k_ref, vt_ref, xr_ref, gy_ref):
    x = x_ref[0]
    ms = jnp.mean(x * x, axis=-1, keepdims=True)
    h = (x * lax.rsqrt(ms + EPS) * g1_ref[...]).astype(BF16)

    n_qkv = ATTN_WIDTH + 2 * KV_WIDTH
    z = jnp.dot(h, w_ref[:, 0:n_qkv], preferred_element_type=F32)

    sq = (z * z).astype(BF16)
    e = e_ref[...]
    sums = [jnp.dot(sq[:, c * MXU_DIM:(c + 1) * MXU_DIM], e, preferred_element_type=F32)
            for c in range(n_qkv // MXU_DIM)]
    ssum = jnp.concatenate(sums, axis=1)
    n_qk = ATTN_WIDTH + KV_WIDTH
    qk = z[:, 0:n_qk] * lax.rsqrt(ssum[:, 0:n_qk] * (1.0 / HEAD_DIM) + EPS) * gqk_ref[...]

    cos = cos_ref[...]
    sa = sa_ref[...]
    sb = sb_ref[...]
    roped = []
    for c in range(n_qk // LANES):
        xc = qk[:, c * LANES:(c + 1) * LANES]
        up = pltpu.roll(xc, LANES - 16, 1)
        dn = pltpu.roll(xc, 16, 1)
        roped.append(xc * cos + up * sa + dn * sb)
    for c in range(ATTN_WIDTH // LANES):
        q_ref[0, :, c * LANES:(c + 1) * LANES] = roped[c].astype(BF16)
    k_ref[0] = roped[ATTN_WIDTH // LANES].astype(BF16)

    vt_ref[0] = z[:, n_qk:n_qkv].T.astype(BF16)

    lru_w = xr_ref.shape[2]
    xr_ref[0] = jnp.dot(h, w_ref[:, n_qkv:n_qkv + lru_w], preferred_element_type=F32)
    yr = jnp.dot(h, w_ref[:, n_qkv + lru_w:n_qkv + 2 * lru_w], preferred_element_type=F32)
    gy_ref[0] = _gelu_tanh(yr).astype(BF16)


def _in_proj(x, g1, w_in, gqk, cos, sa, sb, e):
    B, S, D = x.shape
    tm = PROJ_TM
    n_in = w_in.shape[1]
    lru_w = (n_in - ATTN_WIDTH - 2 * KV_WIDTH) // 2
    n_qk = ATTN_WIDTH + KV_WIDTH
    grid = (B, S // tm)
    tab = pl.BlockSpec((tm, LANES), lambda b, i: (i, 0))
    return pl.pallas_call(
        _in_proj_kernel,
        grid=grid,
        in_specs=[
            pl.BlockSpec((1, tm, D), lambda b, i: (b, i, 0)),
            pl.BlockSpec((1, D), lambda b, i: (0, 0)),
            pl.BlockSpec((D, n_in), lambda b, i: (0, 0)),
            pl.BlockSpec((1, n_qk), lambda b, i: (0, 0)),
            tab, tab, tab,
            pl.BlockSpec((MXU_DIM, MXU_DIM), lambda b, i: (0, 0)),
        ],
        out_specs=[
            pl.BlockSpec((1, tm, ATTN_WIDTH), lambda b, i: (b, i, 0)),
            pl.BlockSpec((1, tm, KV_WIDTH), lambda b, i: (b, i, 0)),
            pl.BlockSpec((1, KV_WIDTH, tm), lambda b, i: (b, 0, i)),
            pl.BlockSpec((1, tm, lru_w), lambda b, i: (b, i, 0)),
            pl.BlockSpec((1, tm, lru_w), lambda b, i: (b, i, 0)),
        ],
        out_shape=[
            jax.ShapeDtypeStruct((B, S, ATTN_WIDTH), BF16),
            jax.ShapeDtypeStruct((B, S, KV_WIDTH), BF16),
            jax.ShapeDtypeStruct((B, KV_WIDTH, S), BF16),
            jax.ShapeDtypeStruct((B, S, lru_w), F32),
            jax.ShapeDtypeStruct((B, S, lru_w), BF16),
        ],
        compiler_params=pltpu.CompilerParams(
            dimension_semantics=("arbitrary", "arbitrary"),
            vmem_limit_bytes=VMEM_LIMIT_BYTES),
        name="in_proj",
    )(x, g1, w_in, gqk, cos, sa, sb, e)


def _attn_kernel(q_ref, k_ref, vt_ref, o_ref):
    tq = q_ref.shape[1]
    S = k_ref.shape[1]
    k = k_ref[0]
    ones = jnp.ones((DEN_ROWS, S), BF16)
    zeros = jnp.zeros((tq, HEAD_DIM), BF16)
    for h in range(N_HEADS):
        j = h // GQA_GROUP
        qh = q_ref[0, :, h * HEAD_DIM:(h + 1) * HEAD_DIM]
        qpad = jnp.concatenate([qh, zeros] if j == 0 else [zeros, qh], axis=1)
        s = lax.dot_general(k, qpad, (((1,), (1,)), ((), ())),
                            preferred_element_type=F32)
        m = jnp.max(s, axis=0, keepdims=True)
        p = jnp.exp2(s - m).astype(BF16)
        vext = jnp.concatenate([vt_ref[0, j * HEAD_DIM:(j + 1) * HEAD_DIM, :], ones], axis=0)
        pv = jnp.dot(vext, p, preferred_element_type=F32)
        o = pv[0:HEAD_DIM] / pv[HEAD_DIM:HEAD_DIM + 1]
        o_ref[0, h * HEAD_DIM:(h + 1) * HEAD_DIM, :] = o.astype(BF16)


def _attention(q, k, vt):
    B, S, _ = q.shape
    tq = ATTN_TQ
    return pl.pallas_call(
        _attn_kernel,
        grid=(B, S // tq),
        in_specs=[
            pl.BlockSpec((1, tq, ATTN_WIDTH), lambda b, i: (b, i, 0)),
            pl.BlockSpec((1, S, KV_WIDTH), lambda b, i: (b, 0, 0)),
            pl.BlockSpec((1, KV_WIDTH, S), lambda b, i: (b, 0, 0)),
        ],
        out_specs=pl.BlockSpec((1, ATTN_WIDTH, tq), lambda b, i: (b, 0, i)),
        out_shape=jax.ShapeDtypeStruct((B, ATTN_WIDTH, S), BF16),
        compiler_params=pltpu.CompilerParams(
            dimension_semantics=("arbitrary", "arbitrary"),
            vmem_limit_bytes=VMEM_LIMIT_BYTES),
        name="attention",
    )(q, k, vt)


def _scan8(a, b, row, reverse):
    for sh in (1, 2, 4):
        if reverse:
            keep = row < (SUBLANES - sh)
            a_s = pltpu.roll(a, SUBLANES - sh, 0)
            b_s = pltpu.roll(b, SUBLANES - sh, 0)
        else:
            keep = row >= sh
            a_s = pltpu.roll(a, sh, 0)
            b_s = pltpu.roll(b, sh, 0)
        b = jnp.where(keep, a * b_s, 0.0) + b
        a = jnp.where(keep, a * a_s, a)
    return a, b


def _lru_kernel(x_ref, gy_ref, cw_ref, cb_ref, wg_ref, bg_ref, lamf_ref, lamb_ref,
                o_ref, xpad, af, bf, ab, bb):
    S = x_ref.shape[1]
    W = x_ref.shape[2]
    tc = LRU_TC
    pad = SUBLANES

    xpad[0:pad, :] = jnp.zeros((pad, W), F32)
    xpad[pad + S:pad + S + pad, :] = jnp.zeros((pad, W), F32)
    xpad[pad:pad + S, :] = x_ref[0]

    def log_sigmoid(v):
        return -(jnp.maximum(-v, 0.0) + jnp.log1p(jnp.exp(-jnp.abs(v))))

    ls_f = LRU_C * log_sigmoid(lamf_ref[...])
    ls_b = LRU_C * log_sigmoid(lamb_ref[...])
    cw = cw_ref[...]
    cb = cb_ref[...]
    bg = bg_ref[0]
    trow = lax.broadcasted_iota(jnp.int32, (tc, W), 0)

    def gates(ci, _):
        t0 = pl.multiple_of(ci * tc, tc)
        xw = xpad[pl.ds(t0, tc + 2 * pad), :]
        xc = (cw[0:1] * xw[pad - 2:pad - 2 + tc] + cw[1:2] * xw[pad - 1:pad - 1 + tc]
              + cw[2:3] * xw[pad:pad + tc] + cw[3:4] * xw[pad + 1:pad + 1 + tc] + cb)
        g = jnp.dot(xc.astype(BF16), wg_ref[0], preferred_element_type=F32) + bg
        t = trow + t0
        for d, (ls, a_s, b_s, first) in enumerate(((ls_f, af, bf, 0), (ls_b, ab, bb, S - 1))):
            r = _sigmoid(g[:, (2 * d) * W:(2 * d + 1) * W])
            i = _sigmoid(g[:, (2 * d + 1) * W:(2 * d + 2) * W])
            log_a = r * ls
            a = jnp.exp(log_a)
            mult = jnp.sqrt(-jnp.tanh(log_a) * (1.0 + a * a))
            mult = jnp.where(t == first, 1.0, mult)
            a_s[pl.ds(t0, tc), :] = a
            b_s[pl.ds(t0, tc), :] = mult * i * xc
        return 0

    lax.fori_loop(0, S // tc, gates, 0)

    row = lax.broadcasted_iota(jnp.int32, (SUBLANES, W), 0)
    n_rows = S // SUBLANES

    def scan(i, carry):
        hf, hb = carry
        tf = pl.multiple_of(i * SUBLANES, SUBLANES)
        tb = pl.multiple_of((n_rows - 1 - i) * SUBLANES, SUBLANES)
        A, Bv = _scan8(af[pl.ds(tf, SUBLANES), :], bf[pl.ds(tf, SUBLANES), :], row, False)
        h = Bv + A * hf
        bf[pl.ds(tf, SUBLANES), :] = h
        hf = jnp.broadcast_to(h[SUBLANES - 1:SUBLANES, :], (SUBLANES, W))
        A, Bv = _scan8(ab[pl.ds(tb, SUBLANES), :], bb[pl.ds(tb, SUBLANES), :], row, True)
        h = Bv + A * hb
        bb[pl.ds(tb, SUBLANES), :] = h
        hb = jnp.broadcast_to(h[0:1, :], (SUBLANES, W))
        return hf, hb

    z8 = jnp.zeros((SUBLANES, W), F32)
    lax.fori_loop(0, n_rows, scan, (z8, z8), unroll=4)

    def combine(ci, _):
        t0 = pl.multiple_of(ci * tc, tc)
        hs = bf[pl.ds(t0, tc), :] + bb[pl.ds(t0, tc), :]
        o_ref[0, pl.ds(t0, tc), :] = (hs * gy_ref[0, pl.ds(t0, tc), :].astype(F32)).astype(BF16)
        return 0

    lax.fori_loop(0, S // tc, combine, 0)


def _lru(xr, gy, conv_w, conv_b, wg, bg, lam_f, lam_b):
    B, S, C = xr.shape
    W = LANES
    n_grp = C // W
    seq = pl.BlockSpec((1, S, W), lambda b, c: (b, 0, c))
    vec = pl.BlockSpec((1, W), lambda b, c: (0, c))
    return pl.pallas_call(
        _lru_kernel,
        grid=(B, n_grp),
        in_specs=[
            seq, seq,
            pl.BlockSpec((conv_w.shape[0], W), lambda b, c: (0, c)),
            vec,
            pl.BlockSpec((1, W, 4 * W), lambda b, c: (c, 0, 0)),
            pl.BlockSpec((1, 1, 4 * W), lambda b, c: (c, 0, 0)),
            vec, vec,
        ],
        out_specs=seq,
        out_shape=jax.ShapeDtypeStruct((B, S, C), BF16),
        scratch_shapes=[
            pltpu.VMEM((S + 2 * SUBLANES, W), F32),
            pltpu.VMEM((S, W), F32), pltpu.VMEM((S, W), F32),
            pltpu.VMEM((S, W), F32), pltpu.VMEM((S, W), F32),
        ],
        compiler_params=pltpu.CompilerParams(
            dimension_semantics=("arbitrary", "arbitrary"),
            vmem_limit_bytes=VMEM_LIMIT_BYTES),
        name="rg_lru",
    )(xr, gy, conv_w, conv_b, wg, bg, lam_f, lam_b)


def _out_proj_kernel(ot_ref, lru_ref, x_ref, wa_ref, wl_ref, g2_ref, x1_ref, h2_ref):
    mixed = lax.dot_general(ot_ref[0], wa_ref[...], (((0,), (0,)), ((), ())),
                            preferred_element_type=F32)
    mixed = mixed + jnp.dot(lru_ref[0], wl_ref[...], preferred_element_type=F32)
    x1 = x_ref[0] + mixed
    x1_ref[0] = x1
    ms = jnp.mean(x1 * x1, axis=-1, keepdims=True)
    h2_ref[0] = (x1 * lax.rsqrt(ms + EPS) * g2_ref[...]).astype(BF16)


def _out_proj(ot, lru, x, wo_a, wo_l, g2):
    B, S, D = x.shape
    tm = PROJ_TM
    return pl.pallas_call(
        _out_proj_kernel,
        grid=(B, S // tm),
        in_specs=[
            pl.BlockSpec((1, ATTN_WIDTH, tm), lambda b, i: (b, 0, i)),
            pl.BlockSpec((1, tm, lru.shape[2]), lambda b, i: (b, i, 0)),
            pl.BlockSpec((1, tm, D), lambda b, i: (b, i, 0)),
            pl.BlockSpec(wo_a.shape, lambda b, i: (0, 0)),
            pl.BlockSpec(wo_l.shape, lambda b, i: (0, 0)),
            pl.BlockSpec((1, D), lambda b, i: (0, 0)),
        ],
        out_specs=[
            pl.BlockSpec((1, tm, D), lambda b, i: (b, i, 0)),
            pl.BlockSpec((1, tm, D), lambda b, i: (b, i, 0)),
        ],
        out_shape=[
            jax.ShapeDtypeStruct((B, S, D), F32),
            jax.ShapeDtypeStruct((B, S, D), BF16),
        ],
        compiler_params=pltpu.CompilerParams(
            dimension_semantics=("arbitrary", "arbitrary"),
            vmem_limit_bytes=VMEM_LIMIT_BYTES),
        name="out_proj",
    )(ot, lru, x, wo_a, wo_l, g2)


def _ffn_kernel(hp_ref, h_ref, hn_ref, x1_ref, wg_ref, wv_ref, cwg_ref, cwv_ref, cbg_ref, cbv_ref,
                wd_ref, gf_ref, o_ref, acc_ref):
    i = pl.program_id(1)
    j = pl.program_id(2)
    tm = h_ref.shape[1]
    halo = FFN_HALO
    hcat = jnp.concatenate([hp_ref[0], h_ref[0], hn_ref[0]], axis=0)
    row = lax.broadcasted_iota(jnp.int32, (tm, 1), 0)
    first = row == jnp.where(i == 0, 0, -1)
    last = row == jnp.where(i == pl.num_programs(1) - 1, tm - 1, -1)

    def conv(w_ref, cw_ref, cb_ref):
        u = jnp.dot(hcat, w_ref[...], preferred_element_type=F32)
        cw = cw_ref[...]
        prev = jnp.where(first, 0.0, u[halo - 1:halo - 1 + tm])
        nxt = jnp.where(last, 0.0, u[halo + 1:halo + 1 + tm])
        return cw[0:1] * prev + cw[1:2] * u[halo:halo + tm] + cw[2:3] * nxt + cb_ref[...]

    gate = conv(wg_ref, cwg_ref, cbg_ref)
    val = conv(wv_ref, cwv_ref, cbv_ref)
    act = (_gelu_tanh(gate) * val).astype(BF16)
    part = jnp.dot(act, wd_ref[...], preferred_element_type=F32)

    @pl.when(j == 0)
    def _():
        acc_ref[...] = part

    @pl.when(j > 0)
    def _():
        acc_ref[...] += part

    @pl.when(j == pl.num_programs(2) - 1)
    def _():
        x2 = x1_ref[0] + acc_ref[...]
        ms = jnp.mean(x2 * x2, axis=-1, keepdims=True)
        o_ref[0] = x2 * lax.rsqrt(ms + EPS) * gf_ref[...]


def _ffn(h2, x1, w_up, cw, cb, w_down, gf):
    B, S, D = x1.shape
    tm, fc, halo = FFN_TM, FFN_FC, FFN_HALO
    d_ff = w_down.shape[0]
    n_ff = d_ff // fc
    hb = tm // halo
    n_hb = S // halo
    return pl.pallas_call(
        _ffn_kernel,
        grid=(B, S // tm, n_ff),
        in_specs=[
            pl.BlockSpec((1, halo, D), lambda b, i, j: (b, jnp.maximum(i * hb - 1, 0), 0)),
            pl.BlockSpec((1, tm, D), lambda b, i, j: (b, i, 0)),
            pl.BlockSpec((1, halo, D), lambda b, i, j: (b, jnp.minimum((i + 1) * hb, n_hb - 1), 0)),
            pl.BlockSpec((1, tm, D), lambda b, i, j: (b, i, 0)),
            pl.BlockSpec((D, fc), lambda b, i, j: (0, j)),
            pl.BlockSpec((D, fc), lambda b, i, j: (0, n_ff + j)),
            pl.BlockSpec((cw.shape[0], fc), lambda b, i, j: (0, j)),
            pl.BlockSpec((cw.shape[0], fc), lambda b, i, j: (0, n_ff + j)),
            pl.BlockSpec((1, fc), lambda b, i, j: (0, j)),
            pl.BlockSpec((1, fc), lambda b, i, j: (0, n_ff + j)),
            pl.BlockSpec((fc, D), lambda b, i, j: (j, 0)),
            pl.BlockSpec((1, D), lambda b, i, j: (0, 0)),
        ],
        out_specs=pl.BlockSpec((1, tm, D), lambda b, i, j: (b, i, 0)),
        out_shape=jax.ShapeDtypeStruct((B, S, D), F32),
        scratch_shapes=[pltpu.VMEM((tm, D), F32)],
        compiler_params=pltpu.CompilerParams(
            dimension_semantics=("arbitrary", "arbitrary", "arbitrary"),
            vmem_limit_bytes=VMEM_LIMIT_BYTES),
        name="conv_ffn",
    )(h2, h2, h2, x1, w_up, w_up, cw, cw, cb, cb, w_down, gf)


def _rope_tables(S):
    t = jnp.arange(S, dtype=jnp.int32)
    pos = jnp.stack([t // GRID_W, t % GRID_W], axis=1).astype(F32)
    quarter = HEAD_DIM // 4
    inv_freq = 1.0 / (ROPE_THETA ** (jnp.arange(0, 2 * quarter, 2, dtype=F32) / (2 * quarter)))
    ang = pos[:, :, None] * inv_freq[None, None, :]
    cos = jnp.cos(ang)
    sin = jnp.sin(ang)
    zero = jnp.zeros_like(sin)
    cos_h = jnp.concatenate([cos, cos], axis=2).reshape(S, HEAD_DIM)
    sa_h = jnp.concatenate([-sin, zero], axis=2).reshape(S, HEAD_DIM)
    sb_h = jnp.concatenate([zero, sin], axis=2).reshape(S, HEAD_DIM)
    rep = LANES // HEAD_DIM
    return (jnp.tile(cos_h, (1, rep)), jnp.tile(sa_h, (1, rep)), jnp.tile(sb_h, (1, rep)))


def _block_diag_pairs(w):
    nb, bs, _ = w.shape
    w2 = w.reshape(nb // 2, 2, bs, bs)
    z = jnp.zeros((nb // 2, bs, bs), w.dtype)
    top = jnp.concatenate([w2[:, 0], z], axis=2)
    bot = jnp.concatenate([z, w2[:, 1]], axis=2)
    return jnp.concatenate([top, bot], axis=1)


def kernel(x, norm1_g, w_in, q_norm_g, k_norm_g, lru_conv_w, lru_conv_b, wa_f, ba_f, wx_f, bx_f, lam_f,
           wa_b, ba_b, wx_b, bx_b, lam_b, w_out, norm2_g, w_up, up_conv_w, up_conv_b, w_down, final_g):
    B, S, D = x.shape
    depth = w_in.shape[0]
    assert depth == 1, "the final RMSNorm is fused into the (single) layer's channel mixer"
    lru_w = lam_f.shape[1]
    n_grp = lru_w // LANES
    cos, sa, sb = _rope_tables(S)
    idx = jnp.arange(MXU_DIM) // HEAD_DIM
    e = (idx[:, None] == idx[None, :]).astype(BF16)

    for l in range(depth):
        gqk = jnp.concatenate([jnp.tile(q_norm_g[l] * (HEAD_DIM ** -0.5 * math.log2(math.e)), N_HEADS),
                               jnp.tile(k_norm_g[l], N_KV_HEADS)])[None, :]
        q, k, vt, xr, gy = _in_proj(x, norm1_g[l][None, :], w_in[l].astype(BF16), gqk, cos, sa, sb, e)
        ot = _attention(q, k, vt)

        wg = jnp.concatenate([_block_diag_pairs(w[l]) for w in (wa_f, wx_f, wa_b, wx_b)],
                             axis=2).astype(BF16)
        bg = jnp.concatenate([b[l].reshape(n_grp, 1, LANES) for b in (ba_f, bx_f, ba_b, bx_b)],
                             axis=2)
        lru = _lru(xr, gy, lru_conv_w[l], lru_conv_b[l][None, :], wg, bg,
                   lam_f[l][None, :], lam_b[l][None, :])

        wo = w_out[l].astype(BF16)
        x1, h2 = _out_proj(ot, lru, x, wo[:ATTN_WIDTH], wo[ATTN_WIDTH:], norm2_g[l][None, :])
        x = _ffn(h2, x1, w_up[l].astype(BF16), up_conv_w[l], up_conv_b[l][None, :],
                 w_down[l].astype(BF16), final_g[None, :])
    return x
```

```python
import functools
import math

import jax
import jax.numpy as jnp
from jax import lax
from jax.experimental import pallas as pl
from jax.experimental.pallas import tpu as pltpu

F32 = jnp.float32
BF16 = jnp.bfloat16

N_HEADS = 8
N_KV_HEADS = 2
HEAD_DIM = 64
GQA_GROUP = N_HEADS // N_KV_HEADS
ATTN_WIDTH = N_HEADS * HEAD_DIM
KV_WIDTH = N_KV_HEADS * HEAD_DIM
LRU_BLOCK = 64
LRU_C = 8.0
GRID_W = 64
ROPE_THETA = 10000.0
EPS = 1e-6

LANES = 128
SUBLANES = 8
BF16_ROWS = 16
MXU_DIM = 256
VMEM_LIMIT_BYTES = 56 * 1024 * 1024

PROJ_TM = 512
ATTN_TQ = 256
ATTN_KC = 256
ATTN_MAX_BOUND = 40.0
ATTN_BOUND_MARGIN = 1.03
DEN_ROWS = BF16_ROWS
LRU_TC = 256
FFN_TM = 1024
FFN_FC = 512
FFN_HALO = BF16_ROWS


def _gelu_tanh(x):
    return 0.5 * x * (1.0 + jnp.tanh(math.sqrt(2.0 / math.pi) * (x + 0.044715 * (x * x * x))))


def _sigmoid(x):
    return 0.5 * jnp.tanh(0.5 * x) + 0.5


def _in_proj_kernel(x_ref, g1_ref, w_ref, gqk_ref, cos_ref, sa_ref, sb_ref, e_ref,
                    q_ref, k_ref, vt_ref, xr_ref, gy_ref):
    x = x_ref[0]
    ms = jnp.mean(x * x, axis=-1, keepdims=True)
    h = (x * lax.rsqrt(ms + EPS) * g1_ref[...]).astype(BF16)

    n_qkv = ATTN_WIDTH + 2 * KV_WIDTH
    z = jnp.dot(h, w_ref[:, 0:n_qkv], preferred_element_type=F32)

    sq = (z * z).astype(BF16)
    e = e_ref[...]
    sums = [jnp.dot(sq[:, c * MXU_DIM:(c + 1) * MXU_DIM], e, preferred_element_type=F32)
            for c in range(n_qkv // MXU_DIM)]
    ssum = jnp.concatenate(sums, axis=1)
    n_qk = ATTN_WIDTH + KV_WIDTH
    qk = z[:, 0:n_qk] * lax.rsqrt(ssum[:, 0:n_qk] * (1.0 / HEAD_DIM) + EPS) * gqk_ref[...]

    cos = cos_ref[...]
    sa = sa_ref[...]
    sb = sb_ref[...]
    roped = []
    for c in range(n_qk // LANES):
        xc = qk[:, c * LANES:(c + 1) * LANES]
        up = pltpu.roll(xc, LANES - 16, 1)
        dn = pltpu.roll(xc, 16, 1)
        roped.append(xc * cos + up * sa + dn * sb)
    for c in range(ATTN_WIDTH // LANES):
        q_ref[0, :, c * LANES:(c + 1) * LANES] = roped[c].astype(BF16)
    k_ref[0] = roped[ATTN_WIDTH // LANES].astype(BF16)

    vt_ref[0] = z[:, n_qk:n_qkv].T.astype(BF16)

    lru_w = xr_ref.shape[2]
    xr_ref[0] = jnp.dot(h, w_ref[:, n_qkv:n_qkv + lru_w], preferred_element_type=F32)
    yr = jnp.dot(h, w_ref[:, n_qkv + lru_w:n_qkv + 2 * lru_w], preferred_element_type=F32)
    gy_ref[0] = _gelu_tanh(yr).astype(BF16)


def _in_proj(x, g1, w_in, gqk, cos, sa, sb, e):
    B, S, D = x.shape
    tm = PROJ_TM
    n_in = w_in.shape[1]
    lru_w = (n_in - ATTN_WIDTH - 2 * KV_WIDTH) // 2
    n_qk = ATTN_WIDTH + KV_WIDTH
    grid = (B, S // tm)
    tab = pl.BlockSpec((tm, LANES), lambda b, i: (i, 0))
    return pl.pallas_call(
        _in_proj_kernel,
        grid=grid,
        in_specs=[
            pl.BlockSpec((1, tm, D), lambda b, i: (b, i, 0)),
            pl.BlockSpec((1, D), lambda b, i: (0, 0)),
            pl.BlockSpec((D, n_in), lambda b, i: (0, 0)),
            pl.BlockSpec((1, n_qk), lambda b, i: (0, 0)),
            tab, tab, tab,
            pl.BlockSpec((MXU_DIM, MXU_DIM), lambda b, i: (0, 0)),
        ],
        out_specs=[
            pl.BlockSpec((1, tm, ATTN_WIDTH), lambda b, i: (b, i, 0)),
            pl.BlockSpec((1, tm, KV_WIDTH), lambda b, i: (b, i, 0)),
            pl.BlockSpec((1, KV_WIDTH, tm), lambda b, i: (b, 0, i)),
            pl.BlockSpec((1, tm, lru_w), lambda b, i: (b, i, 0)),
            pl.BlockSpec((1, tm, lru_w), lambda b, i: (b, i, 0)),
        ],
        out_shape=[
            jax.ShapeDtypeStruct((B, S, ATTN_WIDTH), BF16),
            jax.ShapeDtypeStruct((B, S, KV_WIDTH), BF16),
            jax.ShapeDtypeStruct((B, KV_WIDTH, S), BF16),
            jax.ShapeDtypeStruct((B, S, lru_w), F32),
            jax.ShapeDtypeStruct((B, S, lru_w), BF16),
        ],
        compiler_params=pltpu.CompilerParams(
            dimension_semantics=("arbitrary", "arbitrary"),
            vmem_limit_bytes=VMEM_LIMIT_BYTES),
        name="in_proj",
    )(x, g1, w_in, gqk, cos, sa, sb, e)


def _attn_kernel(q_ref, k_ref, vt_ref, o_ref):
    tq = q_ref.shape[1]
    S = k_ref.shape[1]
    k = k_ref[0]
    ones = jnp.ones((DEN_ROWS, S), BF16)
    zeros = jnp.zeros((tq, HEAD_DIM), BF16)
    for h in range(N_HEADS):
        j = h // GQA_GROUP
        qh = q_ref[0, :, h * HEAD_DIM:(h + 1) * HEAD_DIM]
        qpad = jnp.concatenate([qh, zeros] if j == 0 else [zeros, qh], axis=1)
        s = lax.dot_general(k, qpad, (((1,), (1,)), ((), ())),
                            preferred_element_type=F32)
        m = jnp.max(s, axis=0, keepdims=True)
        p = jnp.exp2(s - m).astype(BF16)
        vext = jnp.concatenate([vt_ref[0, j * HEAD_DIM:(j + 1) * HEAD_DIM, :], ones], axis=0)
        pv = jnp.dot(vext, p, preferred_element_type=F32)
        o = pv[0:HEAD_DIM] / pv[HEAD_DIM:HEAD_DIM + 1]
        o_ref[0, h * HEAD_DIM:(h + 1) * HEAD_DIM, :] = o.astype(BF16)


def _attn_bounded_kernel(c_ref, q_ref, k_ref, vt_ref, o_ref):
    tq = q_ref.shape[1]
    S = k_ref.shape[1]
    c = c_ref[0]
    ones = jnp.ones((DEN_ROWS, S), BF16)
    zeros = jnp.zeros((tq, HEAD_DIM), BF16)
    pair = 2
    for j in range(N_KV_HEADS):
        vext = jnp.concatenate([vt_ref[0, j * HEAD_DIM:(j + 1) * HEAD_DIM, :], ones], axis=0)
        for h0 in range(j * GQA_GROUP, (j + 1) * GQA_GROUP, pair):
            qpads = []
            for h in range(h0, h0 + pair):
                qh = q_ref[0, :, h * HEAD_DIM:(h + 1) * HEAD_DIM]
                qpads.append(jnp.concatenate([qh, zeros] if j == 0 else [zeros, qh], axis=1))
            qpad = jnp.concatenate(qpads, axis=0)
            acc = jnp.zeros((HEAD_DIM + DEN_ROWS, pair * tq), F32)
            def scores(ci, qpad=qpad):
                kc = k_ref[0, ci * ATTN_KC:(ci + 1) * ATTN_KC, :]
                return lax.dot_general(kc, qpad, (((1,), (1,)), ((), ())),
                                       preferred_element_type=F32)

            n_chunks = S // ATTN_KC
            s_next = scores(0)
            for ci in range(n_chunks):
                s = s_next
                if ci + 1 < n_chunks:
                    s_next = scores(ci + 1)
                p = jnp.exp2(s - c).astype(BF16)
                acc = acc + jnp.dot(vext[:, ci * ATTN_KC:(ci + 1) * ATTN_KC], p,
                                    preferred_element_type=F32)
            o = (acc[0:HEAD_DIM] / acc[HEAD_DIM:HEAD_DIM + 1]).astype(BF16)
            for t in range(pair):
                o_ref[0, (h0 + t) * HEAD_DIM:(h0 + t + 1) * HEAD_DIM, :] = o[:, t * tq:(t + 1) * tq]


def _attention(q, k, vt, c):
    B, S, _ = q.shape
    tq = ATTN_TQ
    specs = [
        pl.BlockSpec((1, tq, ATTN_WIDTH), lambda b, i: (b, i, 0)),
        pl.BlockSpec((1, S, KV_WIDTH), lambda b, i: (b, 0, 0)),
        pl.BlockSpec((1, KV_WIDTH, S), lambda b, i: (b, 0, 0)),
    ]
    common = dict(
        grid=(B, S // tq),
        out_specs=pl.BlockSpec((1, ATTN_WIDTH, tq), lambda b, i: (b, 0, i)),
        out_shape=jax.ShapeDtypeStruct((B, ATTN_WIDTH, S), BF16),
        compiler_params=pltpu.CompilerParams(
            dimension_semantics=("arbitrary", "arbitrary"),
            vmem_limit_bytes=VMEM_LIMIT_BYTES),
    )
    bounded = pl.pallas_call(
        _attn_bounded_kernel,
        in_specs=[pl.BlockSpec(memory_space=pltpu.SMEM)] + specs,
        name="attention_bounded", **common)
    exact = pl.pallas_call(_attn_kernel, in_specs=specs, name="attention", **common)
    return lax.cond(c[0] <= ATTN_MAX_BOUND,
                    lambda: bounded(c, q, k, vt),
                    lambda: exact(q, k, vt))


def _scan8(a, b, row, reverse):
    for sh in (1, 2, 4):
        if reverse:
            keep = row < (SUBLANES - sh)
            a_s = pltpu.roll(a, SUBLANES - sh, 0)
            b_s = pltpu.roll(b, SUBLANES - sh, 0)
        else:
            keep = row >= sh
            a_s = pltpu.roll(a, sh, 0)
            b_s = pltpu.roll(b, sh, 0)
        b = jnp.where(keep, a * b_s, 0.0) + b
        a = jnp.where(keep, a * a_s, a)
    return a, b


def _lru_kernel(x_ref, gy_ref, cw_ref, cb_ref, wg_ref, bg_ref, lamf_ref, lamb_ref,
                o_ref, xpad, af, bf, ab, bb):
    S = x_ref.shape[1]
    W = x_ref.shape[2]
    tc = LRU_TC
    pad = SUBLANES

    xpad[0:pad, :] = jnp.zeros((pad, W), F32)
    xpad[pad + S:pad + S + pad, :] = jnp.zeros((pad, W), F32)
    xpad[pad:pad + S, :] = x_ref[0]

    def log_sigmoid(v):
        return -(jnp.maximum(-v, 0.0) + jnp.log1p(jnp.exp(-jnp.abs(v))))

    ls_f = LRU_C * log_sigmoid(lamf_ref[...])
    ls_b = LRU_C * log_sigmoid(lamb_ref[...])
    cw = cw_ref[...]
    cb = cb_ref[...]
    bg = bg_ref[0]
    trow = lax.broadcasted_iota(jnp.int32, (tc, W), 0)

    def gates(ci, _):
        t0 = pl.multiple_of(ci * tc, tc)
        xw = xpad[pl.ds(t0, tc + 2 * pad), :]
        xc = (cw[0:1] * xw[pad - 2:pad - 2 + tc] + cw[1:2] * xw[pad - 1:pad - 1 + tc]
              + cw[2:3] * xw[pad:pad + tc] + cw[3:4] * xw[pad + 1:pad + 1 + tc] + cb)
        g = jnp.dot(xc.astype(BF16), wg_ref[0], preferred_element_type=F32) + bg
        t = trow + t0
        for d, (ls, a_s, b_s, first) in enumerate(((ls_f, af, bf, 0), (ls_b, ab, bb, S - 1))):
            r = _sigmoid(g[:, (2 * d) * W:(2 * d + 1) * W])
            i = _sigmoid(g[:, (2 * d + 1) * W:(2 * d + 2) * W])
            log_a = r * ls
            a = jnp.exp(log_a)
            mult = jnp.sqrt(-jnp.tanh(log_a) * (1.0 + a * a))
            mult = jnp.where(t == first, 1.0, mult)
            a_s[pl.ds(t0, tc), :] = a
            b_s[pl.ds(t0, tc), :] = mult * i * xc
        return 0

    lax.fori_loop(0, S // tc, gates, 0)

    row = lax.broadcasted_iota(jnp.int32, (SUBLANES, W), 0)
    n_rows = S // SUBLANES

    def scan(i, carry):
        hf, hb = carry
        tf = pl.multiple_of(i * SUBLANES, SUBLANES)
        tb = pl.multiple_of((n_rows - 1 - i) * SUBLANES, SUBLANES)
        A, Bv = _scan8(af[pl.ds(tf, SUBLANES), :], bf[pl.ds(tf, SUBLANES), :], row, False)
        h = Bv + A * hf
        bf[pl.ds(tf, SUBLANES), :] = h
        hf = jnp.broadcast_to(h[SUBLANES - 1:SUBLANES, :], (SUBLANES, W))
        A, Bv = _scan8(ab[pl.ds(tb, SUBLANES), :], bb[pl.ds(tb, SUBLANES), :], row, True)
        h = Bv + A * hb
        bb[pl.ds(tb, SUBLANES), :] = h
        hb = jnp.broadcast_to(h[0:1, :], (SUBLANES, W))
        return hf, hb

    z8 = jnp.zeros((SUBLANES, W), F32)
    lax.fori_loop(0, n_rows, scan, (z8, z8), unroll=4)

    def combine(ci, _):
        t0 = pl.multiple_of(ci * tc, tc)
        hs = bf[pl.ds(t0, tc), :] + bb[pl.ds(t0, tc), :]
        o_ref[0, pl.ds(t0, tc), :] = (hs * gy_ref[0, pl.ds(t0, tc), :].astype(F32)).astype(BF16)
        return 0

    lax.fori_loop(0, S // tc, combine, 0)


def _lru(xr, gy, conv_w, conv_b, wg, bg, lam_f, lam_b):
    B, S, C = xr.shape
    W = LANES
    n_grp = C // W
    seq = pl.BlockSpec((1, S, W), lambda b, c: (b, 0, c))
    vec = pl.BlockSpec((1, W), lambda b, c: (0, c))
    return pl.pallas_call(
        _lru_kernel,
        grid=(B, n_grp),
        in_specs=[
            seq, seq,
            pl.BlockSpec((conv_w.shape[0], W), lambda b, c: (0, c)),
            vec,
            pl.BlockSpec((1, W, 4 * W), lambda b, c: (c, 0, 0)),
            pl.BlockSpec((1, 1, 4 * W), lambda b, c: (c, 0, 0)),
            vec, vec,
        ],
        out_specs=seq,
        out_shape=jax.ShapeDtypeStruct((B, S, C), BF16),
        scratch_shapes=[
            pltpu.VMEM((S + 2 * SUBLANES, W), F32),
            pltpu.VMEM((S, W), F32), pltpu.VMEM((S, W), F32),
            pltpu.VMEM((S, W), F32), pltpu.VMEM((S, W), F32),
        ],
        compiler_params=pltpu.CompilerParams(
            dimension_semantics=("arbitrary", "arbitrary"),
            vmem_limit_bytes=VMEM_LIMIT_BYTES),
        name="rg_lru",
    )(xr, gy, conv_w, conv_b, wg, bg, lam_f, lam_b)


def _out_proj_kernel(ot_ref, lru_ref, x_ref, wa_ref, wl_ref, g2_ref, x1_ref, h2_ref):
    mixed = lax.dot_general(ot_ref[0], wa_ref[...], (((0,), (0,)), ((), ())),
                            preferred_element_type=F32)
    mixed = mixed + jnp.dot(lru_ref[0], wl_ref[...], preferred_element_type=F32)
    x1 = x_ref[0] + mixed
    x1_ref[0] = x1
    ms = jnp.mean(x1 * x1, axis=-1, keepdims=True)
    h2_ref[0] = (x1 * lax.rsqrt(ms + EPS) * g2_ref[...]).astype(BF16)


def _out_proj(ot, lru, x, wo_a, wo_l, g2):
    B, S, D = x.shape
    tm = PROJ_TM
    return pl.pallas_call(
        _out_proj_kernel,
        grid=(B, S // tm),
        in_specs=[
            pl.BlockSpec((1, ATTN_WIDTH, tm), lambda b, i: (b, 0, i)),
            pl.BlockSpec((1, tm, lru.shape[2]), lambda b, i: (b, i, 0)),
            pl.BlockSpec((1, tm, D), lambda b, i: (b, i, 0)),
            pl.BlockSpec(wo_a.shape, lambda b, i: (0, 0)),
            pl.BlockSpec(wo_l.shape, lambda b, i: (0, 0)),
            pl.BlockSpec((1, D), lambda b, i: (0, 0)),
        ],
        out_specs=[
            pl.BlockSpec((1, tm, D), lambda b, i: (b, i, 0)),
            pl.BlockSpec((1, tm, D), lambda b, i: (b, i, 0)),
        ],
        out_shape=[
            jax.ShapeDtypeStruct((B, S, D), F32),
            jax.ShapeDtypeStruct((B, S, D), BF16),
        ],
        compiler_params=pltpu.CompilerParams(
            dimension_semantics=("arbitrary", "arbitrary"),
            vmem_limit_bytes=VMEM_LIMIT_BYTES),
        name="out_proj",
    )(ot, lru, x, wo_a, wo_l, g2)


def _ffn_kernel(hp_ref, h_ref, hn_ref, x1_ref, wg_ref, wv_ref, cwg_ref, cwv_ref, cbg_ref, cbv_ref,
                wd_ref, gf_ref, o_ref, acc_ref):
    i = pl.program_id(1)
    j = pl.program_id(2)
    tm = h_ref.shape[1]
    halo = FFN_HALO
    hcat = jnp.concatenate([hp_ref[0], h_ref[0], hn_ref[0]], axis=0)
    row = lax.broadcasted_iota(jnp.int32, (tm, 1), 0)
    first = row == jnp.where(i == 0, 0, -1)
    last = row == jnp.where(i == pl.num_programs(1) - 1, tm - 1, -1)

    def conv(w_ref, cw_ref, cb_ref):
        u = jnp.dot(hcat, w_ref[...], preferred_element_type=F32)
        cw = cw_ref[...]
        prev = jnp.where(first, 0.0, u[halo - 1:halo - 1 + tm])
        nxt = jnp.where(last, 0.0, u[halo + 1:halo + 1 + tm])
        return cw[0:1] * prev + cw[1:2] * u[halo:halo + tm] + cw[2:3] * nxt + cb_ref[...]

    gate = conv(wg_ref, cwg_ref, cbg_ref)
    val = conv(wv_ref, cwv_ref, cbv_ref)
    act = (_gelu_tanh(gate) * val).astype(BF16)
    part = jnp.dot(act, wd_ref[...], preferred_element_type=F32)

    @pl.when(j == 0)
    def _():
        acc_ref[...] = part

    @pl.when(j > 0)
    def _():
        acc_ref[...] += part

    @pl.when(j == pl.num_programs(2) - 1)
    def _():
        x2 = x1_ref[0] + acc_ref[...]
        ms = jnp.mean(x2 * x2, axis=-1, keepdims=True)
        o_ref[0] = x2 * lax.rsqrt(ms + EPS) * gf_ref[...]


def _ffn(h2, x1, w_up, cw, cb, w_down, gf):
    B, S, D = x1.shape
    tm, fc, halo = FFN_TM, FFN_FC, FFN_HALO
    d_ff = w_down.shape[0]
    n_ff = d_ff // fc
    hb = tm // halo
    n_hb = S // halo
    return pl.pallas_call(
        _ffn_kernel,
        grid=(B, S // tm, n_ff),
        in_specs=[
            pl.BlockSpec((1, halo, D), lambda b, i, j: (b, jnp.maximum(i * hb - 1, 0), 0)),
            pl.BlockSpec((1, tm, D), lambda b, i, j: (b, i, 0)),
            pl.BlockSpec((1, halo, D), lambda b, i, j: (b, jnp.minimum((i + 1) * hb, n_hb - 1), 0)),
            pl.BlockSpec((1, tm, D), lambda b, i, j: (b, i, 0)),
            pl.BlockSpec((D, fc), lambda b, i, j: (0, j)),
            pl.BlockSpec((D, fc), lambda b, i, j: (0, n_ff + j)),
            pl.BlockSpec((cw.shape[0], fc), lambda b, i, j: (0, j)),
            pl.BlockSpec((cw.shape[0], fc), lambda b, i, j: (0, n_ff + j)),
            pl.BlockSpec((1, fc), lambda b, i, j: (0, j)),
            pl.BlockSpec((1, fc), lambda b, i, j: (0, n_ff + j)),
            pl.BlockSpec((fc, D), lambda b, i, j: (j, 0)),
            pl.BlockSpec((1, D), lambda b, i, j: (0, 0)),
        ],
        out_specs=pl.BlockSpec((1, tm, D), lambda b, i, j: (b, i, 0)),
        out_shape=jax.ShapeDtypeStruct((B, S, D), F32),
        scratch_shapes=[pltpu.VMEM((tm, D), F32)],
        compiler_params=pltpu.CompilerParams(
            dimension_semantics=("arbitrary", "arbitrary", "arbitrary"),
            vmem_limit_bytes=VMEM_LIMIT_BYTES),
        name="conv_ffn",
    )(h2, h2, h2, x1, w_up, w_up, cw, cw, cb, cb, w_down, gf)


def _rope_tables(S):
    t = jnp.arange(S, dtype=jnp.int32)
    pos = jnp.stack([t // GRID_W, t % GRID_W], axis=1).astype(F32)
    quarter = HEAD_DIM // 4
    inv_freq = 1.0 / (ROPE_THETA ** (jnp.arange(0, 2 * quarter, 2, dtype=F32) / (2 * quarter)))
    ang = pos[:, :, None] * inv_freq[None, None, :]
    cos = jnp.cos(ang)
    sin = jnp.sin(ang)
    zero = jnp.zeros_like(sin)
    cos_h = jnp.concatenate([cos, cos], axis=2).reshape(S, HEAD_DIM)
    sa_h = jnp.concatenate([-sin, zero], axis=2).reshape(S, HEAD_DIM)
    sb_h = jnp.concatenate([zero, sin], axis=2).reshape(S, HEAD_DIM)
    rep = LANES // HEAD_DIM
    return (jnp.tile(cos_h, (1, rep)), jnp.tile(sa_h, (1, rep)), jnp.tile(sb_h, (1, rep)))


def _block_diag_pairs(w):
    nb, bs, _ = w.shape
    w2 = w.reshape(nb // 2, 2, bs, bs)
    z = jnp.zeros((nb // 2, bs, bs), w.dtype)
    top = jnp.concatenate([w2[:, 0], z], axis=2)
    bot = jnp.concatenate([z, w2[:, 1]], axis=2)
    return jnp.concatenate([top, bot], axis=1)


def kernel(x, norm1_g, w_in, q_norm_g, k_norm_g, lru_conv_w, lru_conv_b, wa_f, ba_f, wx_f, bx_f, lam_f,
           wa_b, ba_b, wx_b, bx_b, lam_b, w_out, norm2_g, w_up, up_conv_w, up_conv_b, w_down, final_g):
    B, S, D = x.shape
    depth = w_in.shape[0]
    assert depth == 1, "the final RMSNorm is fused into the (single) layer's channel mixer"
    lru_w = lam_f.shape[1]
    n_grp = lru_w // LANES
    cos, sa, sb = _rope_tables(S)
    idx = jnp.arange(MXU_DIM) // HEAD_DIM
    e = (idx[:, None] == idx[None, :]).astype(BF16)

    for l in range(depth):
        gqk = jnp.concatenate([jnp.tile(q_norm_g[l] * (HEAD_DIM ** -0.5 * math.log2(math.e)), N_HEADS),
                               jnp.tile(k_norm_g[l], N_KV_HEADS)])[None, :]
        q, k, vt, xr, gy = _in_proj(x, norm1_g[l][None, :], w_in[l].astype(BF16), gqk, cos, sa, sb, e)
        c = (ATTN_BOUND_MARGIN * HEAD_DIM) * jnp.max(jnp.abs(gqk[:, :ATTN_WIDTH])) \
            * jnp.max(jnp.abs(gqk[:, ATTN_WIDTH:]))
        ot = _attention(q, k, vt, c.reshape(1))

        wg = jnp.concatenate([_block_diag_pairs(w[l]) for w in (wa_f, wx_f, wa_b, wx_b)],
                             axis=2).astype(BF16)
        bg = jnp.concatenate([b[l].reshape(n_grp, 1, LANES) for b in (ba_f, bx_f, ba_b, bx_b)],
                             axis=2)
        lru = _lru(xr, gy, lru_conv_w[l], lru_conv_b[l][None, :], wg, bg,
                   lam_f[l][None, :], lam_b[l][None, :])

        wo = w_out[l].astype(BF16)
        x1, h2 = _out_proj(ot, lru, x, wo[:ATTN_WIDTH], wo[ATTN_WIDTH:], norm2_g[l][None, :])
        x = _ffn(h2, x1, w_up[l].astype(BF16), up_conv_w[l], up_conv_b[l][None, :],
                 w_down[l].astype(BF16), final_g[None, :])
    return x
```

```python
import math

import jax
import jax.numpy as jnp
from jax import lax
from jax.experimental import pallas as pl
from jax.experimental.pallas import tpu as pltpu

F32 = jnp.float32
BF16 = jnp.bfloat16

N_HEADS = 8
N_KV_HEADS = 2
HEAD_DIM = 64
GQA_GROUP = N_HEADS // N_KV_HEADS
ATTN_WIDTH = N_HEADS * HEAD_DIM
KV_WIDTH = N_KV_HEADS * HEAD_DIM
LRU_C = 8.0
GRID_W = 64
ROPE_THETA = 10000.0
EPS = 1e-6

LANES = 128
SUBLANES = 8
BF16_ROWS = 16
MXU_DIM = 256
VMEM_LIMIT_BYTES = 56 * 1024 * 1024

PROJ_TM = 512
ATTN_TQ = 256
ATTN_KC = 256
ATTN_MAX_BOUND = 40.0
ATTN_BOUND_MARGIN = 1.03
DEN_ROWS = BF16_ROWS
LRU_TC = 256
FFN_TM = 512
FFN_SUB = 256


def _gelu_tanh(x):
    return 0.5 * x * (1.0 + jnp.tanh(math.sqrt(2.0 / math.pi) * (x + 0.044715 * (x * x * x))))


def _in_proj_kernel(x_ref, g1_ref, w_ref, gqk_ref, cos_ref, sa_ref, sb_ref, e_ref,
                    q_ref, k_ref, vt_ref, xr_ref, gy_ref):
    x = x_ref[0]
    ms = jnp.mean(x * x, axis=-1, keepdims=True)
    h = (x * lax.rsqrt(ms + EPS) * g1_ref[...]).astype(BF16)

    n_qkv = ATTN_WIDTH + 2 * KV_WIDTH
    z = jnp.dot(h, w_ref[:, 0:n_qkv], preferred_element_type=F32)

    sq = (z * z).astype(BF16)
    e = e_ref[...]
    sums = [jnp.dot(sq[:, c * MXU_DIM:(c + 1) * MXU_DIM], e, preferred_element_type=F32)
            for c in range(n_qkv // MXU_DIM)]
    ssum = jnp.concatenate(sums, axis=1)
    n_qk = ATTN_WIDTH + KV_WIDTH
    qk = z[:, 0:n_qk] * lax.rsqrt(ssum[:, 0:n_qk] * (1.0 / HEAD_DIM) + EPS) * gqk_ref[...]

    cos = cos_ref[...]
    sa = sa_ref[...]
    sb = sb_ref[...]
    roped = []
    for c in range(n_qk // LANES):
        xc = qk[:, c * LANES:(c + 1) * LANES]
        up = pltpu.roll(xc, LANES - 16, 1)
        dn = pltpu.roll(xc, 16, 1)
        roped.append(xc * cos + up * sa + dn * sb)
    for c in range(ATTN_WIDTH // LANES):
        q_ref[0, :, c * LANES:(c + 1) * LANES] = roped[c].astype(BF16)
    k_ref[0] = roped[ATTN_WIDTH // LANES].astype(BF16)

    vt_ref[0] = z[:, n_qk:n_qkv].T.astype(BF16)

    lru_w = xr_ref.shape[2]
    xr_ref[0] = jnp.dot(h, w_ref[:, n_qkv:n_qkv + lru_w], preferred_element_type=F32)
    yr = jnp.dot(h, w_ref[:, n_qkv + lru_w:n_qkv + 2 * lru_w], preferred_element_type=F32)
    gy_ref[0] = _gelu_tanh(yr).astype(BF16)


def _in_proj(x, g1, w_in, gqk, cos, sa, sb, e):
    B, S, D = x.shape
    tm = PROJ_TM
    n_in = w_in.shape[1]
    lru_w = (n_in - ATTN_WIDTH - 2 * KV_WIDTH) // 2
    n_qk = ATTN_WIDTH + KV_WIDTH
    grid = (B, S // tm)
    tab = pl.BlockSpec((tm, LANES), lambda b, i: (i, 0))
    return pl.pallas_call(
        _in_proj_kernel,
        grid=grid,
        in_specs=[
            pl.BlockSpec((1, tm, D), lambda b, i: (b, i, 0)),
            pl.BlockSpec((1, D), lambda b, i: (0, 0)),
            pl.BlockSpec((D, n_in), lambda b, i: (0, 0)),
            pl.BlockSpec((1, n_qk), lambda b, i: (0, 0)),
            tab, tab, tab,
            pl.BlockSpec((MXU_DIM, MXU_DIM), lambda b, i: (0, 0)),
        ],
        out_specs=[
            pl.BlockSpec((1, tm, ATTN_WIDTH), lambda b, i: (b, i, 0)),
            pl.BlockSpec((1, tm, KV_WIDTH), lambda b, i: (b, i, 0)),
            pl.BlockSpec((1, KV_WIDTH, tm), lambda b, i: (b, 0, i)),
            pl.BlockSpec((1, tm, lru_w), lambda b, i: (b, i, 0)),
            pl.BlockSpec((1, tm, lru_w), lambda b, i: (b, i, 0)),
        ],
        out_shape=[
            jax.ShapeDtypeStruct((B, S, ATTN_WIDTH), BF16),
            jax.ShapeDtypeStruct((B, S, KV_WIDTH), BF16),
            jax.ShapeDtypeStruct((B, KV_WIDTH, S), BF16),
            jax.ShapeDtypeStruct((B, S, lru_w), F32),
            jax.ShapeDtypeStruct((B, S, lru_w), BF16),
        ],
        compiler_params=pltpu.CompilerParams(
            dimension_semantics=("arbitrary", "arbitrary"),
            vmem_limit_bytes=VMEM_LIMIT_BYTES),
        name="in_proj",
    )(x, g1, w_in, gqk, cos, sa, sb, e)


def _attn_kernel(q_ref, k_ref, vt_ref, o_ref):
    tq = q_ref.shape[1]
    S = k_ref.shape[1]
    k = k_ref[0]
    ones = jnp.ones((DEN_ROWS, S), BF16)
    zeros = jnp.zeros((tq, HEAD_DIM), BF16)
    for h in range(N_HEADS):
        j = h // GQA_GROUP
        qh = q_ref[0, :, h * HEAD_DIM:(h + 1) * HEAD_DIM]
        qpad = jnp.concatenate([qh, zeros] if j == 0 else [zeros, qh], axis=1)
        s = lax.dot_general(k, qpad, (((1,), (1,)), ((), ())),
                            preferred_element_type=F32)
        m = jnp.max(s, axis=0, keepdims=True)
        p = jnp.exp2(s - m).astype(BF16)
        vext = jnp.concatenate([vt_ref[0, j * HEAD_DIM:(j + 1) * HEAD_DIM, :], ones], axis=0)
        pv = jnp.dot(vext, p, preferred_element_type=F32)
        o = pv[0:HEAD_DIM] / pv[HEAD_DIM:HEAD_DIM + 1]
        o_ref[0, h * HEAD_DIM:(h + 1) * HEAD_DIM, :] = o.astype(BF16)


def _attn_bounded_kernel(c_ref, q_ref, k_ref, vt_ref, o_ref):
    tq = q_ref.shape[1]
    S = k_ref.shape[1]
    c = c_ref[0]
    ones = jnp.ones((DEN_ROWS, S), BF16)
    zeros = jnp.zeros((tq, HEAD_DIM), BF16)
    pair = 2
    for j in range(N_KV_HEADS):
        vext = jnp.concatenate([vt_ref[0, j * HEAD_DIM:(j + 1) * HEAD_DIM, :], ones], axis=0)
        for h0 in range(j * GQA_GROUP, (j + 1) * GQA_GROUP, pair):
            qpads = []
            for h in range(h0, h0 + pair):
                qh = q_ref[0, :, h * HEAD_DIM:(h + 1) * HEAD_DIM]
                qpads.append(jnp.concatenate([qh, zeros] if j == 0 else [zeros, qh], axis=1))
            qpad = jnp.concatenate(qpads, axis=0)
            acc = jnp.zeros((HEAD_DIM + DEN_ROWS, pair * tq), F32)

            def scores(ci, qpad=qpad):
                kc = k_ref[0, ci * ATTN_KC:(ci + 1) * ATTN_KC, :]
                return lax.dot_general(kc, qpad, (((1,), (1,)), ((), ())),
                                       preferred_element_type=F32)

            n_chunks = S // ATTN_KC
            s_next = scores(0)
            for ci in range(n_chunks):
                s = s_next
                if ci + 1 < n_chunks:
                    s_next = scores(ci + 1)
                p = jnp.exp2(s - c).astype(BF16)
                acc = acc + jnp.dot(vext[:, ci * ATTN_KC:(ci + 1) * ATTN_KC], p,
                                    preferred_element_type=F32)
            o = (acc[0:HEAD_DIM] / acc[HEAD_DIM:HEAD_DIM + 1]).astype(BF16)
            for t in range(pair):
                o_ref[0, (h0 + t) * HEAD_DIM:(h0 + t + 1) * HEAD_DIM, :] = o[:, t * tq:(t + 1) * tq]


def _attention(q, k, vt, c):
    B, S, _ = q.shape
    tq = ATTN_TQ
    specs = [
        pl.BlockSpec((1, tq, ATTN_WIDTH), lambda b, i: (b, i, 0)),
        pl.BlockSpec((1, S, KV_WIDTH), lambda b, i: (b, 0, 0)),
        pl.BlockSpec((1, KV_WIDTH, S), lambda b, i: (b, 0, 0)),
    ]
    common = dict(
        grid=(B, S // tq),
        out_specs=pl.BlockSpec((1, ATTN_WIDTH, tq), lambda b, i: (b, 0, i)),
        out_shape=jax.ShapeDtypeStruct((B, ATTN_WIDTH, S), BF16),
        compiler_params=pltpu.CompilerParams(
            dimension_semantics=("arbitrary", "arbitrary"),
            vmem_limit_bytes=VMEM_LIMIT_BYTES),
    )
    bounded = pl.pallas_call(
        _attn_bounded_kernel,
        in_specs=[pl.BlockSpec(memory_space=pltpu.SMEM)] + specs,
        name="attention_bounded", **common)
    exact = pl.pallas_call(_attn_kernel, in_specs=specs, name="attention", **common)
    return lax.cond(c[0] <= ATTN_MAX_BOUND,
                    lambda: bounded(c, q, k, vt),
                    lambda: exact(q, k, vt))


def _lru_kernel(x_ref, gy_ref, cw_ref, cb_ref, wg_ref, bg_ref, lamf_ref, lamb_ref,
                o_ref, xpad, af, bf, ab, bb, cf_ref, cr_ref):
    S = x_ref.shape[1]
    W = x_ref.shape[2]
    tc = LRU_TC
    pad = SUBLANES
    seg = S // SUBLANES
    pitch = seg + SUBLANES

    xpad[0:pad, :] = jnp.zeros((pad, W), F32)
    xpad[pad + S:pad + S + pad, :] = jnp.zeros((pad, W), F32)
    xpad[pad:pad + S, :] = x_ref[0]

    def log_sigmoid(v):
        return -(jnp.maximum(-v, 0.0) + jnp.log1p(jnp.exp(-jnp.abs(v))))

    hl_f = (0.5 * LRU_C) * log_sigmoid(lamf_ref[...])
    hl_b = (0.5 * LRU_C) * log_sigmoid(lamb_ref[...])
    cw = cw_ref[...]
    cb = cb_ref[...]
    bg = bg_ref[0]
    per_seg = seg // tc
    n_chunks = S // tc

    def chunk_row(ci):
        return pl.multiple_of((ci // per_seg) * pitch + (ci % per_seg) * tc, SUBLANES)

    def gates(ci, first_row, last_row):
        t0 = pl.multiple_of(ci * tc, tc)
        xw = xpad[pl.ds(t0, tc + 2 * pad), :]
        xc = (cw[0:1] * xw[pad - 2:pad - 2 + tc] + cw[1:2] * xw[pad - 1:pad - 1 + tc]
              + cw[2:3] * xw[pad:pad + tc] + cw[3:4] * xw[pad + 1:pad + 1 + tc] + cb)
        g = jnp.dot(xc.astype(BF16), wg_ref[0], preferred_element_type=F32) + bg
        xh = 0.5 * xc
        r0 = chunk_row(ci)
        trow = lax.broadcasted_iota(jnp.int32, (tc, 1), 0)
        for d, (hl, a_s, b_s, unit) in enumerate(((hl_f, af, bf, 0 if first_row else None),
                                                  (hl_b, ab, bb, tc - 1 if last_row else None))):
            tr = jnp.tanh(g[:, (2 * d) * W:(2 * d + 1) * W])
            ti = jnp.tanh(g[:, (2 * d + 1) * W:(2 * d + 2) * W])
            log_a = tr * hl + hl
            a = jnp.exp(log_a)
            m2 = jnp.tanh(log_a) * (-1.0 - a * a)
            mult = m2 * lax.rsqrt(jnp.maximum(m2, 1e-30))
            if unit is not None:
                mult = jnp.where(trow == unit, 1.0, mult)
            a_s[pl.ds(r0, tc), :] = a
            b_s[pl.ds(r0, tc), :] = mult * (ti * xh + xh)
        return 0

    gates(0, True, False)
    lax.fori_loop(1, n_chunks - 1, lambda ci, _: gates(ci, False, False), 0, unroll=2)
    gates(n_chunks - 1, False, True)

    def step(a_s, b_s, r, h, A):
        rows = pl.ds(r, SUBLANES, stride=pitch)
        a = a_s[rows, :]
        h = a * h + b_s[rows, :]
        A = a * A
        b_s[rows, :] = h
        a_s[rows, :] = A
        return h, A

    def scan(r, carry):
        hf, Af, hb, Ab = carry
        hf, Af = step(af, bf, r, hf, Af)
        hb, Ab = step(ab, bb, seg - 1 - r, hb, Ab)
        return hf, Af, hb, Ab

    z8 = jnp.zeros((SUBLANES, W), F32)
    o8 = jnp.ones((SUBLANES, W), F32)
    hf, Af, hb, Ab = lax.fori_loop(0, seg, scan, (z8, o8, z8, o8), unroll=8)

    c = jnp.zeros((1, W), F32)
    rows = [c]
    for sgm in range(SUBLANES - 1):
        c = hf[sgm:sgm + 1] + Af[sgm:sgm + 1] * c
        rows.append(c)
    cf_ref[...] = jnp.concatenate(rows, axis=0)
    c = jnp.zeros((1, W), F32)
    rows = [c]
    for sgm in range(SUBLANES - 1, 0, -1):
        c = hb[sgm:sgm + 1] + Ab[sgm:sgm + 1] * c
        rows.append(c)
    cr_ref[...] = jnp.concatenate(rows[::-1], axis=0)

    def combine(ci, _):
        t0 = pl.multiple_of(ci * tc, tc)
        r0 = chunk_row(ci)
        sgm = ci // per_seg
        h_f = bf[pl.ds(r0, tc), :] + af[pl.ds(r0, tc), :] * cf_ref[pl.ds(sgm, 1), :]
        h_b = bb[pl.ds(r0, tc), :] + ab[pl.ds(r0, tc), :] * cr_ref[pl.ds(sgm, 1), :]
        o_ref[0, pl.ds(t0, tc), :] = ((h_f + h_b) * gy_ref[0, pl.ds(t0, tc), :].astype(F32)).astype(BF16)
        return 0

    lax.fori_loop(0, n_chunks, combine, 0)


def _lru(xr, gy, conv_w, conv_b, wg, bg, lam_f, lam_b):
    B, S, C = xr.shape
    W = LANES
    n_grp = C // W
    seq = pl.BlockSpec((1, S, W), lambda b, c: (b, 0, c))
    vec = pl.BlockSpec((1, W), lambda b, c: (0, c))
    seg_rows = S + SUBLANES * SUBLANES
    return pl.pallas_call(
        _lru_kernel,
        grid=(B, n_grp),
        in_specs=[
            seq, seq,
            pl.BlockSpec((conv_w.shape[0], W), lambda b, c: (0, c)),
            vec,
            pl.BlockSpec((1, W, 4 * W), lambda b, c: (c, 0, 0)),
            pl.BlockSpec((1, 1, 4 * W), lambda b, c: (c, 0, 0)),
            vec, vec,
        ],
        out_specs=seq,
        out_shape=jax.ShapeDtypeStruct((B, S, C), BF16),
        scratch_shapes=[
            pltpu.VMEM((S + 2 * SUBLANES, W), F32),
            pltpu.VMEM((seg_rows, W), F32), pltpu.VMEM((seg_rows, W), F32),
            pltpu.VMEM((seg_rows, W), F32), pltpu.VMEM((seg_rows, W), F32),
            pltpu.VMEM((SUBLANES, W), F32), pltpu.VMEM((SUBLANES, W), F32),
        ],
        compiler_params=pltpu.CompilerParams(
            dimension_semantics=("arbitrary", "arbitrary"),
            vmem_limit_bytes=VMEM_LIMIT_BYTES),
        name="rg_lru",
    )(xr, gy, conv_w, conv_b, wg, bg, lam_f, lam_b)


def _out_proj_kernel(ot_ref, lru_ref, x_ref, wa_ref, wl_ref, x1_ref):
    mixed = lax.dot_general(ot_ref[0], wa_ref[...], (((0,), (0,)), ((), ())),
                            preferred_element_type=F32)
    mixed = mixed + jnp.dot(lru_ref[0], wl_ref[...], preferred_element_type=F32)
    x1_ref[0] = x_ref[0] + mixed


def _out_proj(ot, lru, x, wo_a, wo_l):
    B, S, D = x.shape
    tm = PROJ_TM
    return pl.pallas_call(
        _out_proj_kernel,
        grid=(B, S // tm),
        in_specs=[
            pl.BlockSpec((1, ATTN_WIDTH, tm), lambda b, i: (b, 0, i)),
            pl.BlockSpec((1, tm, lru.shape[2]), lambda b, i: (b, i, 0)),
            pl.BlockSpec((1, tm, D), lambda b, i: (b, i, 0)),
            pl.BlockSpec(wo_a.shape, lambda b, i: (0, 0)),
            pl.BlockSpec(wo_l.shape, lambda b, i: (0, 0)),
        ],
        out_specs=pl.BlockSpec((1, tm, D), lambda b, i: (b, i, 0)),
        out_shape=jax.ShapeDtypeStruct((B, S, D), F32),
        compiler_params=pltpu.CompilerParams(
            dimension_semantics=("arbitrary", "arbitrary"),
            vmem_limit_bytes=VMEM_LIMIT_BYTES),
        name="out_proj",
    )(ot, lru, x, wo_a, wo_l)


def _ffn_kernel(xp_ref, x1_ref, xn_ref, g2_ref, wg_ref, wv_ref, cwg_ref, cwv_ref, cbg_ref, cbv_ref,
                wd_ref, gf_ref, o_ref, xs_ref, acc_ref):
    i = pl.program_id(1)
    tm = x1_ref.shape[1]
    gr = tm // SUBLANES
    n_slab = x1_ref.shape[2] // LANES

    def group(s):
        return jnp.concatenate([xs_ref[k, pl.ds(s, gr, stride=SUBLANES), :] for k in range(n_slab)], axis=1)

    def norm(xs, g):
        ms = jnp.mean(xs * xs, axis=-1, keepdims=True)
        return xs * lax.rsqrt(ms + EPS) * g

    g2 = g2_ref[...]
    for k in range(n_slab):
        xs_ref[k] = x1_ref[0, :, k * LANES:(k + 1) * LANES]
    keep_p = jnp.where(i == 0, 0.0, 1.0)
    keep_n = jnp.where(i == pl.num_programs(1) - 1, 0.0, 1.0)
    halo = jnp.concatenate([norm(xp_ref[0], g2) * keep_p, norm(xn_ref[0], g2) * keep_n], axis=0)
    h = jnp.concatenate([norm(group(s), g2).astype(BF16) for s in range(SUBLANES)]
                        + [halo.astype(BF16)], axis=0)
    n_sub = wg_ref.shape[1] // FFN_SUB

    def up(c):
        cols = slice(c * FFN_SUB, (c + 1) * FFN_SUB)
        return (jnp.dot(h, wg_ref[:, cols], preferred_element_type=F32),
                jnp.dot(h, wv_ref[:, cols], preferred_element_type=F32))

    def conv(u, cw, cb):
        grp = [u[s * gr:(s + 1) * gr] for s in range(SUBLANES)]
        t_prev = u[tm + SUBLANES - 1:tm + SUBLANES]
        t_next = u[tm + SUBLANES:tm + SUBLANES + 1]
        before = jnp.concatenate([t_prev, grp[SUBLANES - 1][:gr - 1]], axis=0)
        after = jnp.concatenate([grp[0][1:], t_next], axis=0)
        outs = []
        for s in range(SUBLANES):
            prv = before if s == 0 else grp[s - 1]
            nxt = after if s == SUBLANES - 1 else grp[s + 1]
            outs.append(cw[0:1] * prv + cw[1:2] * grp[s] + cw[2:3] * nxt + cb)
        return outs

    u_next = up(0)
    for c in range(n_sub):
        ug, uv = u_next
        if c + 1 < n_sub:
            u_next = up(c + 1)
        cols = slice(c * FFN_SUB, (c + 1) * FFN_SUB)
        gate = conv(ug, cwg_ref[:, cols], cbg_ref[:, cols])
        val = conv(uv, cwv_ref[:, cols], cbv_ref[:, cols])
        act = jnp.concatenate([(_gelu_tanh(g) * v).astype(BF16) for g, v in zip(gate, val)], axis=0)
        d = jnp.dot(act, wd_ref[c * FFN_SUB:(c + 1) * FFN_SUB, :], preferred_element_type=F32)
        if c == 0:
            acc_ref[...] = d
        else:
            acc_ref[...] += d

    gf = gf_ref[...]
    for s in range(SUBLANES):
        y = norm(group(s) + acc_ref[s * gr:(s + 1) * gr, :], gf)
        for k in range(n_slab):
            xs_ref[k, pl.ds(s, gr, stride=SUBLANES), :] = y[:, k * LANES:(k + 1) * LANES]
    for k in range(n_slab):
        o_ref[0, :, k * LANES:(k + 1) * LANES] = xs_ref[k]


def _ffn(x1, g2, w_up, cw, cb, w_down, gf):
    B, S, D = x1.shape
    tm = FFN_TM
    d_ff = w_down.shape[0]
    hb = tm // SUBLANES
    n_hb = S // SUBLANES
    resident = dict(pipeline_mode=pl.Buffered(1))
    return pl.pallas_call(
        _ffn_kernel,
        grid=(B, S // tm),
        in_specs=[
            pl.BlockSpec((1, SUBLANES, D), lambda b, i: (b, jnp.maximum(i * hb - 1, 0), 0)),
            pl.BlockSpec((1, tm, D), lambda b, i: (b, i, 0)),
            pl.BlockSpec((1, SUBLANES, D), lambda b, i: (b, jnp.minimum((i + 1) * hb, n_hb - 1), 0)),
            pl.BlockSpec((1, D), lambda b, i: (0, 0)),
            pl.BlockSpec((D, d_ff), lambda b, i: (0, 0), **resident),
            pl.BlockSpec((D, d_ff), lambda b, i: (0, 1), **resident),
            pl.BlockSpec((cw.shape[0], d_ff), lambda b, i: (0, 0)),
            pl.BlockSpec((cw.shape[0], d_ff), lambda b, i: (0, 1)),
            pl.BlockSpec((1, d_ff), lambda b, i: (0, 0)),
            pl.BlockSpec((1, d_ff), lambda b, i: (0, 1)),
            pl.BlockSpec((d_ff, D), lambda b, i: (0, 0), **resident),
            pl.BlockSpec((1, D), lambda b, i: (0, 0)),
        ],
        out_specs=pl.BlockSpec((1, tm, D), lambda b, i: (b, i, 0)),
        out_shape=jax.ShapeDtypeStruct((B, S, D), F32),
        scratch_shapes=[pltpu.VMEM((D // LANES, tm, LANES), F32), pltpu.VMEM((tm, D), F32)],
        compiler_params=pltpu.CompilerParams(
            dimension_semantics=("arbitrary", "arbitrary"),
            vmem_limit_bytes=VMEM_LIMIT_BYTES),
        name="conv_ffn",
    )(x1, x1, x1, g2, w_up, w_up, cw, cw, cb, cb, w_down, gf)


def _rope_tables(S):
    t = jnp.arange(S, dtype=jnp.int32)
    pos = jnp.stack([t // GRID_W, t % GRID_W], axis=1).astype(F32)
    quarter = HEAD_DIM // 4
    inv_freq = 1.0 / (ROPE_THETA ** (jnp.arange(0, 2 * quarter, 2, dtype=F32) / (2 * quarter)))
    ang = pos[:, :, None] * inv_freq[None, None, :]
    cos = jnp.cos(ang)
    sin = jnp.sin(ang)
    zero = jnp.zeros_like(sin)
    cos_h = jnp.concatenate([cos, cos], axis=2).reshape(S, HEAD_DIM)
    sa_h = jnp.concatenate([-sin, zero], axis=2).reshape(S, HEAD_DIM)
    sb_h = jnp.concatenate([zero, sin], axis=2).reshape(S, HEAD_DIM)
    rep = LANES // HEAD_DIM
    return (jnp.tile(cos_h, (1, rep)), jnp.tile(sa_h, (1, rep)), jnp.tile(sb_h, (1, rep)))


def _block_diag_pairs(w):
    nb, bs, _ = w.shape
    w2 = w.reshape(nb // 2, 2, bs, bs)
    z = jnp.zeros((nb // 2, bs, bs), w.dtype)
    top = jnp.concatenate([w2[:, 0], z], axis=2)
    bot = jnp.concatenate([z, w2[:, 1]], axis=2)
    return jnp.concatenate([top, bot], axis=1)


def kernel(x, norm1_g, w_in, q_norm_g, k_norm_g, lru_conv_w, lru_conv_b, wa_f, ba_f, wx_f, bx_f, lam_f,
           wa_b, ba_b, wx_b, bx_b, lam_b, w_out, norm2_g, w_up, up_conv_w, up_conv_b, w_down, final_g):
    B, S, D = x.shape
    depth = w_in.shape[0]
    assert depth == 1, "the final RMSNorm is fused into the (single) layer's channel mixer"
    lru_w = lam_f.shape[1]
    n_grp = lru_w // LANES
    cos, sa, sb = _rope_tables(S)
    idx = jnp.arange(MXU_DIM) // HEAD_DIM
    e = (idx[:, None] == idx[None, :]).astype(BF16)

    for l in range(depth):
        gqk = jnp.concatenate([jnp.tile(q_norm_g[l] * (HEAD_DIM ** -0.5 * math.log2(math.e)), N_HEADS),
                               jnp.tile(k_norm_g[l], N_KV_HEADS)])[None, :]
        q, k, vt, xr, gy = _in_proj(x, norm1_g[l][None, :], w_in[l].astype(BF16), gqk, cos, sa, sb, e)
        c = (ATTN_BOUND_MARGIN * HEAD_DIM) * jnp.max(jnp.abs(gqk[:, :ATTN_WIDTH])) \
            * jnp.max(jnp.abs(gqk[:, ATTN_WIDTH:]))
        ot = _attention(q, k, vt, c.reshape(1))

        wg = (0.5 * jnp.concatenate([_block_diag_pairs(w[l]) for w in (wa_f, wx_f, wa_b, wx_b)],
                                    axis=2)).astype(BF16)
        bg = 0.5 * jnp.concatenate([b[l].reshape(n_grp, 1, LANES) for b in (ba_f, bx_f, ba_b, bx_b)],
                                   axis=2)
        lru = _lru(xr, gy, lru_conv_w[l], lru_conv_b[l][None, :], wg, bg,
                   lam_f[l][None, :], lam_b[l][None, :])

        wo = w_out[l].astype(BF16)
        x1 = _out_proj(ot, lru, x, wo[:ATTN_WIDTH], wo[ATTN_WIDTH:])
        x = _ffn(x1, norm2_g[l][None, :], w_up[l].astype(BF16), up_conv_w[l], up_conv_b[l][None, :],
                 w_down[l].astype(BF16), final_g[None, :])
    return x
```

```python
import math

import jax
import jax.numpy as jnp
from jax import lax
from jax.experimental import pallas as pl
from jax.experimental.pallas import tpu as pltpu

F32 = jnp.float32
BF16 = jnp.bfloat16

N_HEADS = 8
N_KV_HEADS = 2
HEAD_DIM = 64
GQA_GROUP = N_HEADS // N_KV_HEADS
ATTN_WIDTH = N_HEADS * HEAD_DIM
KV_WIDTH = N_KV_HEADS * HEAD_DIM
LRU_C = 8.0
GRID_W = 64
ROPE_THETA = 10000.0
EPS = 1e-6

LANES = 128
SUBLANES = 8
BF16_ROWS = 16
MXU_DIM = 256
VMEM_LIMIT_BYTES = 56 * 1024 * 1024

PROJ_TM = 512
ATTN_TQ = 256
ATTN_KC = 256
ATTN_MAX_BOUND = 40.0
ATTN_BOUND_MARGIN = 1.03
DEN_ROWS = BF16_ROWS
LRU_TC = 256
FFN_TM = 256
FFN_SUB = 256


def _gelu_tanh(x):
    return 0.5 * x * (1.0 + jnp.tanh(math.sqrt(2.0 / math.pi) * (x + 0.044715 * (x * x * x))))


def _in_proj_kernel(x_ref, g1_ref, w_ref, gqk_ref, cos_ref, sa_ref, sb_ref, e_ref,
                    q_ref, k_ref, vt_ref, xr_ref, gy_ref):
    x = x_ref[0]
    ms = jnp.mean(x * x, axis=-1, keepdims=True)
    h = (x * lax.rsqrt(ms + EPS) * g1_ref[...]).astype(BF16)

    n_qkv = ATTN_WIDTH + 2 * KV_WIDTH
    z = jnp.dot(h, w_ref[:, 0:n_qkv], preferred_element_type=F32)

    sq = (z * z).astype(BF16)
    e = e_ref[...]
    sums = [jnp.dot(sq[:, c * MXU_DIM:(c + 1) * MXU_DIM], e, preferred_element_type=F32)
            for c in range(n_qkv // MXU_DIM)]
    ssum = jnp.concatenate(sums, axis=1)
    n_qk = ATTN_WIDTH + KV_WIDTH
    qk = z[:, 0:n_qk] * lax.rsqrt(ssum[:, 0:n_qk] * (1.0 / HEAD_DIM) + EPS) * gqk_ref[...]

    cos = cos_ref[...]
    sa = sa_ref[...]
    sb = sb_ref[...]
    roped = []
    for c in range(n_qk // LANES):
        xc = qk[:, c * LANES:(c + 1) * LANES]
        up = pltpu.roll(xc, LANES - 16, 1)
        dn = pltpu.roll(xc, 16, 1)
        roped.append(xc * cos + up * sa + dn * sb)
    for c in range(ATTN_WIDTH // LANES):
        q_ref[0, :, c * LANES:(c + 1) * LANES] = roped[c].astype(BF16)
    k_ref[0] = roped[ATTN_WIDTH // LANES].astype(BF16)

    vt_ref[0] = z[:, n_qk:n_qkv].T.astype(BF16)

    lru_w = xr_ref.shape[2]
    xr_ref[0] = jnp.dot(h, w_ref[:, n_qkv:n_qkv + lru_w], preferred_element_type=F32)
    yr = jnp.dot(h, w_ref[:, n_qkv + lru_w:n_qkv + 2 * lru_w], preferred_element_type=F32)
    gy_ref[0] = _gelu_tanh(yr).astype(BF16)


def _in_proj(x, g1, w_in, gqk, cos, sa, sb, e):
    B, S, D = x.shape
    tm = PROJ_TM
    n_in = w_in.shape[1]
    lru_w = (n_in - ATTN_WIDTH - 2 * KV_WIDTH) // 2
    n_qk = ATTN_WIDTH + KV_WIDTH
    grid = (B, S // tm)
    tab = pl.BlockSpec((tm, LANES), lambda b, i: (i, 0))
    return pl.pallas_call(
        _in_proj_kernel,
        grid=grid,
        in_specs=[
            pl.BlockSpec((1, tm, D), lambda b, i: (b, i, 0)),
            pl.BlockSpec((1, D), lambda b, i: (0, 0)),
            pl.BlockSpec((D, n_in), lambda b, i: (0, 0)),
            pl.BlockSpec((1, n_qk), lambda b, i: (0, 0)),
            tab, tab, tab,
            pl.BlockSpec((MXU_DIM, MXU_DIM), lambda b, i: (0, 0)),
        ],
        out_specs=[
            pl.BlockSpec((1, tm, ATTN_WIDTH), lambda b, i: (b, i, 0)),
            pl.BlockSpec((1, tm, KV_WIDTH), lambda b, i: (b, i, 0)),
            pl.BlockSpec((1, KV_WIDTH, tm), lambda b, i: (b, 0, i)),
            pl.BlockSpec((1, tm, lru_w), lambda b, i: (b, i, 0)),
            pl.BlockSpec((1, tm, lru_w), lambda b, i: (b, i, 0)),
        ],
        out_shape=[
            jax.ShapeDtypeStruct((B, S, ATTN_WIDTH), BF16),
            jax.ShapeDtypeStruct((B, S, KV_WIDTH), BF16),
            jax.ShapeDtypeStruct((B, KV_WIDTH, S), BF16),
            jax.ShapeDtypeStruct((B, S, lru_w), F32),
            jax.ShapeDtypeStruct((B, S, lru_w), BF16),
        ],
        compiler_params=pltpu.CompilerParams(
            dimension_semantics=("arbitrary", "arbitrary"),
            vmem_limit_bytes=VMEM_LIMIT_BYTES),
        name="in_proj",
    )(x, g1, w_in, gqk, cos, sa, sb, e)


def _attn_kernel(q_ref, k_ref, vt_ref, o_ref):
    tq = q_ref.shape[1]
    S = k_ref.shape[1]
    k = k_ref[0]
    ones = jnp.ones((DEN_ROWS, S), BF16)
    zeros = jnp.zeros((tq, HEAD_DIM), BF16)
    for h in range(N_HEADS):
        j = h // GQA_GROUP
        qh = q_ref[0, :, h * HEAD_DIM:(h + 1) * HEAD_DIM]
        qpad = jnp.concatenate([qh, zeros] if j == 0 else [zeros, qh], axis=1)
        s = lax.dot_general(k, qpad, (((1,), (1,)), ((), ())),
                            preferred_element_type=F32)
        m = jnp.max(s, axis=0, keepdims=True)
        p = jnp.exp2(s - m).astype(BF16)
        vext = jnp.concatenate([vt_ref[0, j * HEAD_DIM:(j + 1) * HEAD_DIM, :], ones], axis=0)
        pv = jnp.dot(vext, p, preferred_element_type=F32)
        o = pv[0:HEAD_DIM] / pv[HEAD_DIM:HEAD_DIM + 1]
        o_ref[0, h * HEAD_DIM:(h + 1) * HEAD_DIM, :] = o.astype(BF16)


def _attn_bounded_kernel(c_ref, q_ref, k_ref, vt_ref, o_ref):
    tq = q_ref.shape[1]
    S = k_ref.shape[1]
    c = c_ref[0]
    ones = jnp.ones((DEN_ROWS, S), BF16)
    zeros = jnp.zeros((tq, HEAD_DIM), BF16)
    pair = GQA_GROUP
    for j in range(N_KV_HEADS):
        vext = jnp.concatenate([vt_ref[0, j * HEAD_DIM:(j + 1) * HEAD_DIM, :], ones], axis=0)
        for h0 in range(j * GQA_GROUP, (j + 1) * GQA_GROUP, pair):
            qpads = []
            for h in range(h0, h0 + pair):
                qh = q_ref[0, :, h * HEAD_DIM:(h + 1) * HEAD_DIM]
                qpads.append(jnp.concatenate([qh, zeros] if j == 0 else [zeros, qh], axis=1))
            qpad = jnp.concatenate(qpads, axis=0)
            acc = jnp.zeros((HEAD_DIM + DEN_ROWS, pair * tq), F32)

            def scores(ci, qpad=qpad):
                kc = k_ref[0, ci * ATTN_KC:(ci + 1) * ATTN_KC, :]
                return lax.dot_general(kc, qpad, (((1,), (1,)), ((), ())),
                                       preferred_element_type=F32)

            n_chunks = S // ATTN_KC
            s_next = scores(0)
            for ci in range(n_chunks):
                s = s_next
                if ci + 1 < n_chunks:
                    s_next = scores(ci + 1)
                p = jnp.exp2(s - c).astype(BF16)
                acc = acc + jnp.dot(vext[:, ci * ATTN_KC:(ci + 1) * ATTN_KC], p,
                                    preferred_element_type=F32)
            o = (acc[0:HEAD_DIM] / acc[HEAD_DIM:HEAD_DIM + 1]).astype(BF16)
            for t in range(pair):
                o_ref[0, (h0 + t) * HEAD_DIM:(h0 + t + 1) * HEAD_DIM, :] = o[:, t * tq:(t + 1) * tq]


def _attention(q, k, vt, c):
    B, S, _ = q.shape
    tq = ATTN_TQ
    specs = [
        pl.BlockSpec((1, tq, ATTN_WIDTH), lambda b, i: (b, i, 0)),
        pl.BlockSpec((1, S, KV_WIDTH), lambda b, i: (b, 0, 0)),
        pl.BlockSpec((1, KV_WIDTH, S), lambda b, i: (b, 0, 0)),
    ]
    common = dict(
        grid=(B, S // tq),
        out_specs=pl.BlockSpec((1, ATTN_WIDTH, tq), lambda b, i: (b, 0, i)),
        out_shape=jax.ShapeDtypeStruct((B, ATTN_WIDTH, S), BF16),
        compiler_params=pltpu.CompilerParams(
            dimension_semantics=("arbitrary", "arbitrary"),
            vmem_limit_bytes=VMEM_LIMIT_BYTES),
    )
    bounded = pl.pallas_call(
        _attn_bounded_kernel,
        in_specs=[pl.BlockSpec(memory_space=pltpu.SMEM)] + specs,
        name="attention_bounded", **common)
    exact = pl.pallas_call(_attn_kernel, in_specs=specs, name="attention", **common)
    return lax.cond(c[0] <= ATTN_MAX_BOUND,
                    lambda: bounded(c, q, k, vt),
                    lambda: exact(q, k, vt))


def _lru_kernel(x_ref, gy_ref, cw_ref, cb_ref, wg_ref, bg_ref, lamf_ref, lamb_ref,
                o_ref, xpad, af, bf, ab, bb, hf_s, pf_s, hb_s, pb_s, cf_ref, cr_ref):
    S = x_ref.shape[1]
    W = x_ref.shape[2]
    tc = LRU_TC
    pad = SUBLANES
    seg = S // SUBLANES
    pitch = seg + SUBLANES

    xpad[0:pad, :] = jnp.zeros((pad, W), F32)
    xpad[pad + S:pad + S + pad, :] = jnp.zeros((pad, W), F32)
    xpad[pad:pad + S, :] = x_ref[0]

    def log_sigmoid(v):
        return -(jnp.maximum(-v, 0.0) + jnp.log1p(jnp.exp(-jnp.abs(v))))

    hl_f = (0.5 * LRU_C) * log_sigmoid(lamf_ref[...])
    hl_b = (0.5 * LRU_C) * log_sigmoid(lamb_ref[...])
    cw = cw_ref[...]
    cb = cb_ref[...]
    bg = bg_ref[0]
    per_seg = seg // tc
    n_chunks = S // tc

    def chunk_row(ci):
        return pl.multiple_of((ci // per_seg) * pitch + (ci % per_seg) * tc, SUBLANES)

    def gates(ci, first_row, last_row):
        t0 = pl.multiple_of(ci * tc, tc)
        xw = xpad[pl.ds(t0, tc + 2 * pad), :]
        xc = (cw[0:1] * xw[pad - 2:pad - 2 + tc] + cw[1:2] * xw[pad - 1:pad - 1 + tc]
              + cw[2:3] * xw[pad:pad + tc] + cw[3:4] * xw[pad + 1:pad + 1 + tc] + cb)
        g = jnp.dot(xc.astype(BF16), wg_ref[0], preferred_element_type=F32) + bg
        xh = 0.5 * xc
        r0 = chunk_row(ci)
        trow = lax.broadcasted_iota(jnp.int32, (tc, 1), 0)
        for d, (hl, a_s, b_s, unit) in enumerate(((hl_f, af, bf, 0 if first_row else None),
                                                  (hl_b, ab, bb, tc - 1 if last_row else None))):
            tr = jnp.tanh(g[:, (2 * d) * W:(2 * d + 1) * W])
            ti = jnp.tanh(g[:, (2 * d + 1) * W:(2 * d + 2) * W])
            log_a = tr * hl + hl
            a = jnp.exp(log_a)
            m2 = jnp.tanh(log_a) * (-1.0 - a * a)
            mult = m2 * lax.rsqrt(jnp.maximum(m2, 1e-30))
            if unit is not None:
                mult = jnp.where(trow == unit, 1.0, mult)
            a_s[pl.ds(r0, tc), :] = a
            b_s[pl.ds(r0, tc), :] = mult * (ti * xh + xh)
        return 0

    gates(0, True, False)
    lax.fori_loop(1, n_chunks - 1, lambda ci, _: gates(ci, False, False), 0, unroll=2)
    gates(n_chunks - 1, False, True)

    def step(a_s, b_s, h_s, p_s, r, h, A):
        rows = pl.ds(r, SUBLANES, stride=pitch)
        a = a_s[rows, :]
        h = a * h + b_s[rows, :]
        A = a * A
        h_s[rows, :] = h
        p_s[rows, :] = A
        return h, A

    def scan(r, carry):
        hf, Af, hb, Ab = carry
        hf, Af = step(af, bf, hf_s, pf_s, r, hf, Af)
        hb, Ab = step(ab, bb, hb_s, pb_s, seg - 1 - r, hb, Ab)
        return hf, Af, hb, Ab

    z8 = jnp.zeros((SUBLANES, W), F32)
    o8 = jnp.ones((SUBLANES, W), F32)
    hf, Af, hb, Ab = lax.fori_loop(0, seg, scan, (z8, o8, z8, o8), unroll=8)

    c = jnp.zeros((1, W), F32)
    rows = [c]
    for sgm in range(SUBLANES - 1):
        c = hf[sgm:sgm + 1] + Af[sgm:sgm + 1] * c
        rows.append(c)
    cf_ref[...] = jnp.concatenate(rows, axis=0)
    c = jnp.zeros((1, W), F32)
    rows = [c]
    for sgm in range(SUBLANES - 1, 0, -1):
        c = hb[sgm:sgm + 1] + Ab[sgm:sgm + 1] * c
        rows.append(c)
    cr_ref[...] = jnp.concatenate(rows[::-1], axis=0)

    def combine(ci, _):
        t0 = pl.multiple_of(ci * tc, tc)
        r0 = chunk_row(ci)
        sgm = ci // per_seg
        h_f = hf_s[pl.ds(r0, tc), :] + pf_s[pl.ds(r0, tc), :] * cf_ref[pl.ds(sgm, 1), :]
        h_b = hb_s[pl.ds(r0, tc), :] + pb_s[pl.ds(r0, tc), :] * cr_ref[pl.ds(sgm, 1), :]
        o_ref[0, pl.ds(t0, tc), :] = ((h_f + h_b) * gy_ref[0, pl.ds(t0, tc), :].astype(F32)).astype(BF16)
        return 0

    lax.fori_loop(0, n_chunks, combine, 0)


def _lru(xr, gy, conv_w, conv_b, wg, bg, lam_f, lam_b):
    B, S, C = xr.shape
    W = LANES
    n_grp = C // W
    seq = pl.BlockSpec((1, S, W), lambda b, c: (b, 0, c))
    vec = pl.BlockSpec((1, W), lambda b, c: (0, c))
    seg_rows = S + SUBLANES * SUBLANES
    return pl.pallas_call(
        _lru_kernel,
        grid=(B, n_grp),
        in_specs=[
            seq, seq,
            pl.BlockSpec((conv_w.shape[0], W), lambda b, c: (0, c)),
            vec,
            pl.BlockSpec((1, W, 4 * W), lambda b, c: (c, 0, 0)),
            pl.BlockSpec((1, 1, 4 * W), lambda b, c: (c, 0, 0)),
            vec, vec,
        ],
        out_specs=seq,
        out_shape=jax.ShapeDtypeStruct((B, S, C), BF16),
        scratch_shapes=[
            pltpu.VMEM((S + 2 * SUBLANES, W), F32),
            pltpu.VMEM((seg_rows, W), F32), pltpu.VMEM((seg_rows, W), F32),
            pltpu.VMEM((seg_rows, W), F32), pltpu.VMEM((seg_rows, W), F32),
            pltpu.VMEM((seg_rows, W), F32), pltpu.VMEM((seg_rows, W), F32),
            pltpu.VMEM((seg_rows, W), F32), pltpu.VMEM((seg_rows, W), F32),
            pltpu.VMEM((SUBLANES, W), F32), pltpu.VMEM((SUBLANES, W), F32),
        ],
        compiler_params=pltpu.CompilerParams(
            dimension_semantics=("arbitrary", "arbitrary"),
            vmem_limit_bytes=VMEM_LIMIT_BYTES),
        name="rg_lru",
    )(xr, gy, conv_w, conv_b, wg, bg, lam_f, lam_b)


def _out_proj_kernel(ot_ref, lru_ref, x_ref, wa_ref, wl_ref, x1_ref):
    mixed = lax.dot_general(ot_ref[0], wa_ref[...], (((0,), (0,)), ((), ())),
                            preferred_element_type=F32)
    mixed = mixed + jnp.dot(lru_ref[0], wl_ref[...], preferred_element_type=F32)
    x1_ref[0] = x_ref[0] + mixed


def _out_proj(ot, lru, x, wo_a, wo_l):
    B, S, D = x.shape
    tm = PROJ_TM
    return pl.pallas_call(
        _out_proj_kernel,
        grid=(B, S // tm),
        in_specs=[
            pl.BlockSpec((1, ATTN_WIDTH, tm), lambda b, i: (b, 0, i)),
            pl.BlockSpec((1, tm, lru.shape[2]), lambda b, i: (b, i, 0)),
            pl.BlockSpec((1, tm, D), lambda b, i: (b, i, 0)),
            pl.BlockSpec(wo_a.shape, lambda b, i: (0, 0)),
            pl.BlockSpec(wo_l.shape, lambda b, i: (0, 0)),
        ],
        out_specs=pl.BlockSpec((1, tm, D), lambda b, i: (b, i, 0)),
        out_shape=jax.ShapeDtypeStruct((B, S, D), F32),
        compiler_params=pltpu.CompilerParams(
            dimension_semantics=("arbitrary", "arbitrary"),
            vmem_limit_bytes=VMEM_LIMIT_BYTES),
        name="out_proj",
    )(ot, lru, x, wo_a, wo_l)


def _ffn_kernel(xp_ref, x1_ref, xn_ref, g2_ref, wg_ref, wv_ref, cwg_ref, cwv_ref, cbg_ref, cbv_ref,
                wd_ref, gf_ref, o_ref, xs_ref, acc_ref):
    i = pl.program_id(1)
    tm = x1_ref.shape[1]
    gr = tm // SUBLANES
    n_slab = x1_ref.shape[2] // LANES

    def group(s):
        return jnp.concatenate([xs_ref[k, pl.ds(s, gr, stride=SUBLANES), :] for k in range(n_slab)], axis=1)

    def norm(xs, g):
        ms = jnp.mean(xs * xs, axis=-1, keepdims=True)
        return xs * lax.rsqrt(ms + EPS) * g

    g2 = g2_ref[...]
    for k in range(n_slab):
        xs_ref[k] = x1_ref[0, :, k * LANES:(k + 1) * LANES]
    keep_p = jnp.where(i == 0, 0.0, 1.0)
    keep_n = jnp.where(i == pl.num_programs(1) - 1, 0.0, 1.0)
    halo = jnp.concatenate([norm(xp_ref[0], g2) * keep_p, norm(xn_ref[0], g2) * keep_n], axis=0)
    h = jnp.concatenate([norm(group(s), g2).astype(BF16) for s in range(SUBLANES)]
                        + [halo.astype(BF16)], axis=0)
    n_sub = wg_ref.shape[1] // FFN_SUB

    def up(c):
        cols = slice(c * FFN_SUB, (c + 1) * FFN_SUB)
        return (jnp.dot(h, wg_ref[:, cols], preferred_element_type=F32),
                jnp.dot(h, wv_ref[:, cols], preferred_element_type=F32))

    def conv(u, cw, cb):
        grp = [u[s * gr:(s + 1) * gr] for s in range(SUBLANES)]
        t_prev = u[tm + SUBLANES - 1:tm + SUBLANES]
        t_next = u[tm + SUBLANES:tm + SUBLANES + 1]
        before = jnp.concatenate([t_prev, grp[SUBLANES - 1][:gr - 1]], axis=0)
        after = jnp.concatenate([grp[0][1:], t_next], axis=0)
        outs = []
        for s in range(SUBLANES):
            prv = before if s == 0 else grp[s - 1]
            nxt = after if s == SUBLANES - 1 else grp[s + 1]
            outs.append(cw[0:1] * prv + cw[1:2] * grp[s] + cw[2:3] * nxt + cb)
        return outs

    u_next = up(0)
    for c in range(n_sub):
        ug, uv = u_next
        if c + 1 < n_sub:
            u_next = up(c + 1)
        cols = slice(c * FFN_SUB, (c + 1) * FFN_SUB)
        gate = conv(ug, cwg_ref[:, cols], cbg_ref[:, cols])
        val = conv(uv, cwv_ref[:, cols], cbv_ref[:, cols])
        act = jnp.concatenate([(_gelu_tanh(g) * v).astype(BF16) for g, v in zip(gate, val)], axis=0)
        d = jnp.dot(act, wd_ref[c * FFN_SUB:(c + 1) * FFN_SUB, :], preferred_element_type=F32)
        if c == 0:
            acc_ref[...] = d
        else:
            acc_ref[...] += d

    gf = gf_ref[...]
    for s in range(SUBLANES):
        y = norm(group(s) + acc_ref[s * gr:(s + 1) * gr, :], gf)
        for k in range(n_slab):
            xs_ref[k, pl.ds(s, gr, stride=SUBLANES), :] = y[:, k * LANES:(k + 1) * LANES]
    for k in range(n_slab):
        o_ref[0, :, k * LANES:(k + 1) * LANES] = xs_ref[k]


def _ffn(x1, g2, w_up, cw, cb, w_down, gf):
    B, S, D = x1.shape
    tm = FFN_TM
    d_ff = w_down.shape[0]
    hb = tm // SUBLANES
    n_hb = S // SUBLANES
    resident = dict(pipeline_mode=pl.Buffered(1))
    return pl.pallas_call(
        _ffn_kernel,
        grid=(B, S // tm),
        in_specs=[
            pl.BlockSpec((1, SUBLANES, D), lambda b, i: (b, jnp.maximum(i * hb - 1, 0), 0)),
            pl.BlockSpec((1, tm, D), lambda b, i: (b, i, 0)),
            pl.BlockSpec((1, SUBLANES, D), lambda b, i: (b, jnp.minimum((i + 1) * hb, n_hb - 1), 0)),
            pl.BlockSpec((1, D), lambda b, i: (0, 0)),
            pl.BlockSpec((D, d_ff), lambda b, i: (0, 0), **resident),
            pl.BlockSpec((D, d_ff), lambda b, i: (0, 1), **resident),
            pl.BlockSpec((cw.shape[0], d_ff), lambda b, i: (0, 0)),
            pl.BlockSpec((cw.shape[0], d_ff), lambda b, i: (0, 1)),
            pl.BlockSpec((1, d_ff), lambda b, i: (0, 0)),
            pl.BlockSpec((1, d_ff), lambda b, i: (0, 1)),
            pl.BlockSpec((d_ff, D), lambda b, i: (0, 0), **resident),
            pl.BlockSpec((1, D), lambda b, i: (0, 0)),
        ],
        out_specs=pl.BlockSpec((1, tm, D), lambda b, i: (b, i, 0)),
        out_shape=jax.ShapeDtypeStruct((B, S, D), F32),
        scratch_shapes=[pltpu.VMEM((D // LANES, tm, LANES), F32), pltpu.VMEM((tm, D), F32)],
        compiler_params=pltpu.CompilerParams(
            dimension_semantics=("arbitrary", "arbitrary"),
            vmem_limit_bytes=VMEM_LIMIT_BYTES),
        name="conv_ffn",
    )(x1, x1, x1, g2, w_up, w_up, cw, cw, cb, cb, w_down, gf)


def _rope_tables(S):
    t = jnp.arange(S, dtype=jnp.int32)
    pos = jnp.stack([t // GRID_W, t % GRID_W], axis=1).astype(F32)
    quarter = HEAD_DIM // 4
    inv_freq = 1.0 / (ROPE_THETA ** (jnp.arange(0, 2 * quarter, 2, dtype=F32) / (2 * quarter)))
    ang = pos[:, :, None] * inv_freq[None, None, :]
    cos = jnp.cos(ang)
    sin = jnp.sin(ang)
    zero = jnp.zeros_like(sin)
    cos_h = jnp.concatenate([cos, cos], axis=2).reshape(S, HEAD_DIM)
    sa_h = jnp.concatenate([-sin, zero], axis=2).reshape(S, HEAD_DIM)
    sb_h = jnp.concatenate([zero, sin], axis=2).reshape(S, HEAD_DIM)
    rep = LANES // HEAD_DIM
    return (jnp.tile(cos_h, (1, rep)), jnp.tile(sa_h, (1, rep)), jnp.tile(sb_h, (1, rep)))


def _block_diag_pairs(w):
    nb, bs, _ = w.shape
    w2 = w.reshape(nb // 2, 2, bs, bs)
    z = jnp.zeros((nb // 2, bs, bs), w.dtype)
    top = jnp.concatenate([w2[:, 0], z], axis=2)
    bot = jnp.concatenate([z, w2[:, 1]], axis=2)
    return jnp.concatenate([top, bot], axis=1)


def kernel(x, norm1_g, w_in, q_norm_g, k_norm_g, lru_conv_w, lru_conv_b, wa_f, ba_f, wx_f, bx_f, lam_f,
           wa_b, ba_b, wx_b, bx_b, lam_b, w_out, norm2_g, w_up, up_conv_w, up_conv_b, w_down, final_g):
    B, S, D = x.shape
    depth = w_in.shape[0]
    assert depth == 1, "the final RMSNorm is fused into the (single) layer's channel mixer"
    lru_w = lam_f.shape[1]
    n_grp = lru_w // LANES
    cos, sa, sb = _rope_tables(S)
    idx = jnp.arange(MXU_DIM) // HEAD_DIM
    e = (idx[:, None] == idx[None, :]).astype(BF16)

    for l in range(depth):
        gqk = jnp.concatenate([jnp.tile(q_norm_g[l] * (HEAD_DIM ** -0.5 * math.log2(math.e)), N_HEADS),
                               jnp.tile(k_norm_g[l], N_KV_HEADS)])[None, :]
        q, k, vt, xr, gy = _in_proj(x, norm1_g[l][None, :], w_in[l].astype(BF16), gqk, cos, sa, sb, e)
        c = (ATTN_BOUND_MARGIN * HEAD_DIM) * jnp.max(jnp.abs(gqk[:, :ATTN_WIDTH])) \
            * jnp.max(jnp.abs(gqk[:, ATTN_WIDTH:]))
        ot = _attention(q, k, vt, c.reshape(1))

        wg = (0.5 * jnp.concatenate([_block_diag_pairs(w[l]) for w in (wa_f, wx_f, wa_b, wx_b)],
                                    axis=2)).astype(BF16)
        bg = 0.5 * jnp.concatenate([b[l].reshape(n_grp, 1, LANES) for b in (ba_f, bx_f, ba_b, bx_b)],
                                   axis=2)
        lru = _lru(xr, gy, lru_conv_w[l], lru_conv_b[l][None, :], wg, bg,
                   lam_f[l][None, :], lam_b[l][None, :])

        wo = w_out[l].astype(BF16)
        x1 = _out_proj(ot, lru, x, wo[:ATTN_WIDTH], wo[ATTN_WIDTH:])
        x = _ffn(x1, norm2_g[l][None, :], w_up[l].astype(BF16), up_conv_w[l], up_conv_b[l][None, :],
                 w_down[l].astype(BF16), final_g[None, :])
    return x
```

```python
import math

import jax
import jax.numpy as jnp
from jax import lax
from jax.experimental import pallas as pl
from jax.experimental.pallas import tpu as pltpu

F32 = jnp.float32
BF16 = jnp.bfloat16

N_HEADS = 8
N_KV_HEADS = 2
HEAD_DIM = 64
GQA_GROUP = N_HEADS // N_KV_HEADS
ATTN_WIDTH = N_HEADS * HEAD_DIM
KV_WIDTH = N_KV_HEADS * HEAD_DIM
LRU_C = 8.0
GRID_W = 64
ROPE_THETA = 10000.0
EPS = 1e-6

LANES = 128
SUBLANES = 8
BF16_ROWS = 16
MXU_DIM = 256
VMEM_LIMIT_BYTES = 56 * 1024 * 1024

PROJ_TM = 512
ATTN_TQ = 256
ATTN_KC = 256
ATTN_MAX_BOUND = 40.0
ATTN_BOUND_MARGIN = 1.03
DEN_ROWS = BF16_ROWS
LRU_TC = 256
FFN_TM = 256
FFN_SUB = 256


def _gelu_tanh(x):
    return 0.5 * x * (1.0 + jnp.tanh(math.sqrt(2.0 / math.pi) * (x + 0.044715 * (x * x * x))))


def _in_proj_kernel(x_ref, g1_ref, w_ref, gqk_ref, cos_ref, sa_ref, sb_ref, e_ref,
                    q_ref, k_ref, vt_ref, xr_ref, gy_ref):
    x = x_ref[0]
    ms = jnp.mean(x * x, axis=-1, keepdims=True)
    h = (x * lax.rsqrt(ms + EPS) * g1_ref[...]).astype(BF16)

    n_qkv = ATTN_WIDTH + 2 * KV_WIDTH
    z = jnp.dot(h, w_ref[:, 0:n_qkv], preferred_element_type=F32)

    sq = (z * z).astype(BF16)
    e = e_ref[...]
    sums = [jnp.dot(sq[:, c * MXU_DIM:(c + 1) * MXU_DIM], e, preferred_element_type=F32)
            for c in range(n_qkv // MXU_DIM)]
    ssum = jnp.concatenate(sums, axis=1)
    n_qk = ATTN_WIDTH + KV_WIDTH
    qk = z[:, 0:n_qk] * lax.rsqrt(ssum[:, 0:n_qk] * (1.0 / HEAD_DIM) + EPS) * gqk_ref[...]

    cos = cos_ref[...]
    sa = sa_ref[...]
    sb = sb_ref[...]
    roped = []
    for c in range(n_qk // LANES):
        xc = qk[:, c * LANES:(c + 1) * LANES]
        up = pltpu.roll(xc, LANES - 16, 1)
        dn = pltpu.roll(xc, 16, 1)
        roped.append(xc * cos + up * sa + dn * sb)
    for c in range(ATTN_WIDTH // LANES):
        q_ref[0, c * LANES:(c + 1) * LANES, :] = roped[c].T.astype(BF16)
    k_ref[0] = roped[ATTN_WIDTH // LANES].astype(BF16)

    vt_ref[0] = z[:, n_qk:n_qkv].T.astype(BF16)

    lru_w = xr_ref.shape[2]
    xr_ref[0] = jnp.dot(h, w_ref[:, n_qkv:n_qkv + lru_w], preferred_element_type=F32)
    yr = jnp.dot(h, w_ref[:, n_qkv + lru_w:n_qkv + 2 * lru_w], preferred_element_type=F32)
    gy_ref[0] = _gelu_tanh(yr).astype(BF16)


def _in_proj(x, g1, w_in, gqk, cos, sa, sb, e):
    B, S, D = x.shape
    tm = PROJ_TM
    n_in = w_in.shape[1]
    lru_w = (n_in - ATTN_WIDTH - 2 * KV_WIDTH) // 2
    n_qk = ATTN_WIDTH + KV_WIDTH
    grid = (B, S // tm)
    tab = pl.BlockSpec((tm, LANES), lambda b, i: (i, 0))
    return pl.pallas_call(
        _in_proj_kernel,
        grid=grid,
        in_specs=[
            pl.BlockSpec((1, tm, D), lambda b, i: (b, i, 0)),
            pl.BlockSpec((1, D), lambda b, i: (0, 0)),
            pl.BlockSpec((D, n_in), lambda b, i: (0, 0)),
            pl.BlockSpec((1, n_qk), lambda b, i: (0, 0)),
            tab, tab, tab,
            pl.BlockSpec((MXU_DIM, MXU_DIM), lambda b, i: (0, 0)),
        ],
        out_specs=[
            pl.BlockSpec((1, ATTN_WIDTH, tm), lambda b, i: (b, 0, i)),
            pl.BlockSpec((1, tm, KV_WIDTH), lambda b, i: (b, i, 0)),
            pl.BlockSpec((1, KV_WIDTH, tm), lambda b, i: (b, 0, i)),
            pl.BlockSpec((1, tm, lru_w), lambda b, i: (b, i, 0)),
            pl.BlockSpec((1, tm, lru_w), lambda b, i: (b, i, 0)),
        ],
        out_shape=[
            jax.ShapeDtypeStruct((B, ATTN_WIDTH, S), BF16),
            jax.ShapeDtypeStruct((B, S, KV_WIDTH), BF16),
            jax.ShapeDtypeStruct((B, KV_WIDTH, S), BF16),
            jax.ShapeDtypeStruct((B, S, lru_w), F32),
            jax.ShapeDtypeStruct((B, S, lru_w), BF16),
        ],
        compiler_params=pltpu.CompilerParams(
            dimension_semantics=("arbitrary", "arbitrary"),
            vmem_limit_bytes=VMEM_LIMIT_BYTES),
        name="in_proj",
    )(x, g1, w_in, gqk, cos, sa, sb, e)


def _attn_kernel(qt_ref, k_ref, vt_ref, o_ref):
    tq = qt_ref.shape[2]
    S = k_ref.shape[1]
    k = k_ref[0]
    ones = jnp.ones((DEN_ROWS, S), BF16)
    zeros = jnp.zeros((HEAD_DIM, tq), BF16)
    for h in range(N_HEADS):
        j = h // GQA_GROUP
        qt = qt_ref[0, h * HEAD_DIM:(h + 1) * HEAD_DIM, :]
        qpad = jnp.concatenate([qt, zeros] if j == 0 else [zeros, qt], axis=0)
        s = jnp.dot(k, qpad, preferred_element_type=F32)
        m = jnp.max(s, axis=0, keepdims=True)
        p = jnp.exp2(s - m).astype(BF16)
        vext = jnp.concatenate([vt_ref[0, j * HEAD_DIM:(j + 1) * HEAD_DIM, :], ones], axis=0)
        pv = jnp.dot(vext, p, preferred_element_type=F32)
        o = pv[0:HEAD_DIM] / pv[HEAD_DIM:HEAD_DIM + 1]
        o_ref[0, h * HEAD_DIM:(h + 1) * HEAD_DIM, :] = o.astype(BF16)


def _attn_bounded_kernel(c_ref, qt_ref, k_ref, vt_ref, o_ref):
    tq = qt_ref.shape[2]
    S = k_ref.shape[1]
    assert tq == MXU_DIM and ATTN_KC == MXU_DIM
    c = c_ref[0]
    n_chunks = S // ATTN_KC
    n_mxu = 2
    n_rounds = N_HEADS // n_mxu
    n_items = n_rounds * n_chunks
    tile_entries = MXU_DIM // 4
    score_acc = (0, tile_entries)
    out_acc = (2 * tile_entries, 3 * tile_entries)
    quarter = ATTN_KC // 4
    ones = jnp.ones((DEN_ROWS, ATTN_KC), BF16)
    zeros_q = jnp.zeros((HEAD_DIM, tq), BF16)
    zeros_k = jnp.zeros((quarter, MXU_DIM - KV_WIDTH), BF16)

    def q_rhs(r, m):
        h = r * n_mxu + m
        rows = [zeros_q] * (MXU_DIM // HEAD_DIM)
        rows[h // GQA_GROUP] = qt_ref[0, h * HEAD_DIM:(h + 1) * HEAD_DIM, :]
        return jnp.concatenate(rows, axis=0)

    for m in range(n_mxu):
        pltpu.matmul_push_rhs(q_rhs(0, m), staging_register=0, mxu_index=m)
    p_tiles = [None] * n_mxu
    for t in range(n_items + 2):
        if t >= 2:
            for m in range(n_mxu):
                pltpu.matmul_push_rhs(p_tiles[m], staging_register=1, mxu_index=m)
        new_tiles = [[] for _ in range(n_mxu)]
        for qd in range(4):
            if t < n_items:
                ci = t % n_chunks
                rows = slice(ci * ATTN_KC + qd * quarter, ci * ATTN_KC + (qd + 1) * quarter)
                lhs = jnp.concatenate([k_ref[0, rows, :], zeros_k], axis=1)
                for m in range(n_mxu):
                    pltpu.matmul_acc_lhs(score_acc[t % 2] + qd * (quarter // 4), lhs, mxu_index=m,
                                         load_staged_rhs=0 if qd == 0 else None)
            if 1 <= t <= n_items:
                for m in range(n_mxu):
                    s = pltpu.matmul_pop(score_acc[(t - 1) % 2] + qd * (quarter // 4), (quarter, tq), F32,
                                         mxu_index=m)
                    new_tiles[m].append(jnp.exp2(s - c).astype(BF16))
        if t + 1 < n_items:
            for m in range(n_mxu):
                pltpu.matmul_push_rhs(q_rhs((t + 1) // n_chunks, m), staging_register=0, mxu_index=m)
        if t >= 2:
            r, cj = divmod(t - 2, n_chunks)
            for m in range(n_mxu):
                h = r * n_mxu + m
                j = h // GQA_GROUP
                vext = jnp.concatenate(
                    [vt_ref[0, j * HEAD_DIM:(j + 1) * HEAD_DIM, cj * ATTN_KC:(cj + 1) * ATTN_KC], ones],
                    axis=0)
                pltpu.matmul_acc_lhs(out_acc[r % 2], vext, mxu_index=m, load_staged_rhs=1)
                if cj == n_chunks - 1:
                    acc = pltpu.matmul_pop(out_acc[r % 2], (HEAD_DIM + DEN_ROWS, tq), F32, mxu_index=m)
                    o = acc[0:HEAD_DIM] / acc[HEAD_DIM:HEAD_DIM + 1]
                    o_ref[0, h * HEAD_DIM:(h + 1) * HEAD_DIM, :] = o.astype(BF16)
        if 1 <= t <= n_items:
            p_tiles = [jnp.concatenate(tl, axis=0) for tl in new_tiles]


def _attention(q, k, vt, c):
    B, _, S = q.shape
    tq = ATTN_TQ
    specs = [
        pl.BlockSpec((1, ATTN_WIDTH, tq), lambda b, i: (b, 0, i)),
        pl.BlockSpec((1, S, KV_WIDTH), lambda b, i: (b, 0, 0)),
        pl.BlockSpec((1, KV_WIDTH, S), lambda b, i: (b, 0, 0)),
    ]
    common = dict(
        grid=(B, S // tq),
        out_specs=pl.BlockSpec((1, ATTN_WIDTH, tq), lambda b, i: (b, 0, i)),
        out_shape=jax.ShapeDtypeStruct((B, ATTN_WIDTH, S), BF16),
        compiler_params=pltpu.CompilerParams(
            dimension_semantics=("arbitrary", "arbitrary"),
            vmem_limit_bytes=VMEM_LIMIT_BYTES),
    )
    bounded = pl.pallas_call(
        _attn_bounded_kernel,
        in_specs=[pl.BlockSpec(memory_space=pltpu.SMEM)] + specs,
        name="attention_bounded", **common)
    exact = pl.pallas_call(_attn_kernel, in_specs=specs, name="attention", **common)
    return lax.cond(c[0] <= ATTN_MAX_BOUND,
                    lambda: bounded(c, q, k, vt),
                    lambda: exact(q, k, vt))


def _lru_kernel(x_ref, gy_ref, cw_ref, cb_ref, wg_ref, bg_ref, lamf_ref, lamb_ref,
                o_ref, xpad, af, bf, ab, bb, hf_s, pf_s, hb_s, pb_s, cf_ref, cr_ref):
    S = x_ref.shape[1]
    W = x_ref.shape[2]
    tc = LRU_TC
    pad = SUBLANES
    seg = S // SUBLANES
    pitch = seg + SUBLANES

    xpad[0:pad, :] = jnp.zeros((pad, W), F32)
    xpad[pad + S:pad + S + pad, :] = jnp.zeros((pad, W), F32)
    xpad[pad:pad + S, :] = x_ref[0]

    def log_sigmoid(v):
        return -(jnp.maximum(-v, 0.0) + jnp.log1p(jnp.exp(-jnp.abs(v))))

    hl_f = (0.5 * LRU_C) * log_sigmoid(lamf_ref[...])
    hl_b = (0.5 * LRU_C) * log_sigmoid(lamb_ref[...])
    cw = cw_ref[...]
    cb = cb_ref[...]
    bg = bg_ref[0]
    per_seg = seg // tc
    n_chunks = S // tc

    def chunk_row(ci):
        return pl.multiple_of((ci // per_seg) * pitch + (ci % per_seg) * tc, SUBLANES)

    def gates(ci, first_row, last_row):
        t0 = pl.multiple_of(ci * tc, tc)
        xw = xpad[pl.ds(t0, tc + 2 * pad), :]
        xc = (cw[0:1] * xw[pad - 2:pad - 2 + tc] + cw[1:2] * xw[pad - 1:pad - 1 + tc]
              + cw[2:3] * xw[pad:pad + tc] + cw[3:4] * xw[pad + 1:pad + 1 + tc] + cb)
        g = jnp.dot(xc.astype(BF16), wg_ref[0], preferred_element_type=F32) + bg
        xh = 0.5 * xc
        r0 = chunk_row(ci)
        trow = lax.broadcasted_iota(jnp.int32, (tc, 1), 0)
        for d, (hl, a_s, b_s, unit) in enumerate(((hl_f, af, bf, 0 if first_row else None),
                                                  (hl_b, ab, bb, tc - 1 if last_row else None))):
            tr = jnp.tanh(g[:, (2 * d) * W:(2 * d + 1) * W])
            ti = jnp.tanh(g[:, (2 * d + 1) * W:(2 * d + 2) * W])
            log_a = tr * hl + hl
            a = jnp.exp(log_a)
            m2 = jnp.tanh(log_a) * (-1.0 - a * a)
            mult = m2 * lax.rsqrt(jnp.maximum(m2, 1e-30))
            if unit is not None:
                mult = jnp.where(trow == unit, 1.0, mult)
            a_s[pl.ds(r0, tc), :] = a
            b_s[pl.ds(r0, tc), :] = mult * (ti * xh + xh)
        return 0

    gates(0, True, False)
    lax.fori_loop(1, n_chunks - 1, lambda ci, _: gates(ci, False, False), 0, unroll=2)
    gates(n_chunks - 1, False, True)

    def step(a_s, b_s, h_s, p_s, r, h, A):
        rows = pl.ds(r, SUBLANES, stride=pitch)
        a = a_s[rows, :]
        h = a * h + b_s[rows, :]
        A = a * A
        h_s[rows, :] = h
        p_s[rows, :] = A
        return h, A

    def scan(r, carry):
        hf, Af, hb, Ab = carry
        hf, Af = step(af, bf, hf_s, pf_s, r, hf, Af)
        hb, Ab = step(ab, bb, hb_s, pb_s, seg - 1 - r, hb, Ab)
        return hf, Af, hb, Ab

    z8 = jnp.zeros((SUBLANES, W), F32)
    o8 = jnp.ones((SUBLANES, W), F32)
    hf, Af, hb, Ab = lax.fori_loop(0, seg, scan, (z8, o8, z8, o8), unroll=8)

    c = jnp.zeros((1, W), F32)
    rows = [c]
    for sgm in range(SUBLANES - 1):
        c = hf[sgm:sgm + 1] + Af[sgm:sgm + 1] * c
        rows.append(c)
    cf_ref[...] = jnp.concatenate(rows, axis=0)
    c = jnp.zeros((1, W), F32)
    rows = [c]
    for sgm in range(SUBLANES - 1, 0, -1):
        c = hb[sgm:sgm + 1] + Ab[sgm:sgm + 1] * c
        rows.append(c)
    cr_ref[...] = jnp.concatenate(rows[::-1], axis=0)

    def combine(ci, _):
        t0 = pl.multiple_of(ci * tc, tc)
        r0 = chunk_row(ci)
        sgm = ci // per_seg
        h_f = hf_s[pl.ds(r0, tc), :] + pf_s[pl.ds(r0, tc), :] * cf_ref[pl.ds(sgm, 1), :]
        h_b = hb_s[pl.ds(r0, tc), :] + pb_s[pl.ds(r0, tc), :] * cr_ref[pl.ds(sgm, 1), :]
        o_ref[0, pl.ds(t0, tc), :] = ((h_f + h_b) * gy_ref[0, pl.ds(t0, tc), :].astype(F32)).astype(BF16)
        return 0

    lax.fori_loop(0, n_chunks, combine, 0)


def _lru(xr, gy, conv_w, conv_b, wg, bg, lam_f, lam_b):
    B, S, C = xr.shape
    W = LANES
    n_grp = C // W
    seq = pl.BlockSpec((1, S, W), lambda b, c: (b, 0, c))
    vec = pl.BlockSpec((1, W), lambda b, c: (0, c))
    seg_rows = S + SUBLANES * SUBLANES
    return pl.pallas_call(
        _lru_kernel,
        grid=(B, n_grp),
        in_specs=[
            seq, seq,
            pl.BlockSpec((conv_w.shape[0], W), lambda b, c: (0, c)),
            vec,
            pl.BlockSpec((1, W, 4 * W), lambda b, c: (c, 0, 0)),
            pl.BlockSpec((1, 1, 4 * W), lambda b, c: (c, 0, 0)),
            vec, vec,
        ],
        out_specs=seq,
        out_shape=jax.ShapeDtypeStruct((B, S, C), BF16),
        scratch_shapes=[
            pltpu.VMEM((S + 2 * SUBLANES, W), F32),
            pltpu.VMEM((seg_rows, W), F32), pltpu.VMEM((seg_rows, W), F32),
            pltpu.VMEM((seg_rows, W), F32), pltpu.VMEM((seg_rows, W), F32),
            pltpu.VMEM((seg_rows, W), F32), pltpu.VMEM((seg_rows, W), F32),
            pltpu.VMEM((seg_rows, W), F32), pltpu.VMEM((seg_rows, W), F32),
            pltpu.VMEM((SUBLANES, W), F32), pltpu.VMEM((SUBLANES, W), F32),
        ],
        compiler_params=pltpu.CompilerParams(
            dimension_semantics=("arbitrary", "arbitrary"),
            vmem_limit_bytes=VMEM_LIMIT_BYTES),
        name="rg_lru",
    )(xr, gy, conv_w, conv_b, wg, bg, lam_f, lam_b)


def _out_proj_kernel(ot_ref, lru_ref, x_ref, wa_ref, wl_ref, x1_ref):
    mixed = lax.dot_general(ot_ref[0], wa_ref[...], (((0,), (0,)), ((), ())),
                            preferred_element_type=F32)
    mixed = mixed + jnp.dot(lru_ref[0], wl_ref[...], preferred_element_type=F32)
    x1_ref[0] = x_ref[0] + mixed


def _out_proj(ot, lru, x, wo_a, wo_l):
    B, S, D = x.shape
    tm = PROJ_TM
    return pl.pallas_call(
        _out_proj_kernel,
        grid=(B, S // tm),
        in_specs=[
            pl.BlockSpec((1, ATTN_WIDTH, tm), lambda b, i: (b, 0, i)),
            pl.BlockSpec((1, tm, lru.shape[2]), lambda b, i: (b, i, 0)),
            pl.BlockSpec((1, tm, D), lambda b, i: (b, i, 0)),
            pl.BlockSpec(wo_a.shape, lambda b, i: (0, 0)),
            pl.BlockSpec(wo_l.shape, lambda b, i: (0, 0)),
        ],
        out_specs=pl.BlockSpec((1, tm, D), lambda b, i: (b, i, 0)),
        out_shape=jax.ShapeDtypeStruct((B, S, D), F32),
        compiler_params=pltpu.CompilerParams(
            dimension_semantics=("arbitrary", "arbitrary"),
            vmem_limit_bytes=VMEM_LIMIT_BYTES),
        name="out_proj",
    )(ot, lru, x, wo_a, wo_l)


def _ffn_kernel(xp_ref, x1_ref, xn_ref, g2_ref, wg_ref, wv_ref, cwg_ref, cwv_ref, cbg_ref, cbv_ref,
                wd_ref, gf_ref, o_ref, xs_ref, acc_ref):
    i = pl.program_id(1)
    tm = x1_ref.shape[1]
    gr = tm // SUBLANES
    n_slab = x1_ref.shape[2] // LANES

    def group(s):
        return jnp.concatenate([xs_ref[k, pl.ds(s, gr, stride=SUBLANES), :] for k in range(n_slab)], axis=1)

    def norm(xs, g):
        ms = jnp.mean(xs * xs, axis=-1, keepdims=True)
        return xs * lax.rsqrt(ms + EPS) * g

    g2 = g2_ref[...]
    for k in range(n_slab):
        xs_ref[k] = x1_ref[0, :, k * LANES:(k + 1) * LANES]
    keep_p = jnp.where(i == 0, 0.0, 1.0)
    keep_n = jnp.where(i == pl.num_programs(1) - 1, 0.0, 1.0)
    halo = jnp.concatenate([norm(xp_ref[0], g2) * keep_p, norm(xn_ref[0], g2) * keep_n], axis=0)
    h = jnp.concatenate([norm(group(s), g2).astype(BF16) for s in range(SUBLANES)]
                        + [halo.astype(BF16)], axis=0)
    n_sub = wg_ref.shape[1] // FFN_SUB

    def up(c):
        cols = slice(c * FFN_SUB, (c + 1) * FFN_SUB)
        return (jnp.dot(h, wg_ref[:, cols], preferred_element_type=F32),
                jnp.dot(h, wv_ref[:, cols], preferred_element_type=F32))

    def conv(u, cw, cb):
        grp = [u[s * gr:(s + 1) * gr] for s in range(SUBLANES)]
        t_prev = u[tm + SUBLANES - 1:tm + SUBLANES]
        t_next = u[tm + SUBLANES:tm + SUBLANES + 1]
        before = jnp.concatenate([t_prev, grp[SUBLANES - 1][:gr - 1]], axis=0)
        after = jnp.concatenate([grp[0][1:], t_next], axis=0)
        outs = []
        for s in range(SUBLANES):
            prv = before if s == 0 else grp[s - 1]
            nxt = after if s == SUBLANES - 1 else grp[s + 1]
            outs.append(cw[0:1] * prv + cw[1:2] * grp[s] + cw[2:3] * nxt + cb)
        return outs

    u_next = up(0)
    for c in range(n_sub):
        ug, uv = u_next
        if c + 1 < n_sub:
            u_next = up(c + 1)
        cols = slice(c * FFN_SUB, (c + 1) * FFN_SUB)
        gate = conv(ug, cwg_ref[:, cols], cbg_ref[:, cols])
        val = conv(uv, cwv_ref[:, cols], cbv_ref[:, cols])
        act = jnp.concatenate([(_gelu_tanh(g) * v).astype(BF16) for g, v in zip(gate, val)], axis=0)
        d = jnp.dot(act, wd_ref[c * FFN_SUB:(c + 1) * FFN_SUB, :], preferred_element_type=F32)
        if c == 0:
            acc_ref[...] = d
        else:
            acc_ref[...] += d

    gf = gf_ref[...]
    for s in range(SUBLANES):
        y = norm(group(s) + acc_ref[s * gr:(s + 1) * gr, :], gf)
        for k in range(n_slab):
            xs_ref[k, pl.ds(s, gr, stride=SUBLANES), :] = y[:, k * LANES:(k + 1) * LANES]
    for k in range(n_slab):
        o_ref[0, :, k * LANES:(k + 1) * LANES] = xs_ref[k]


def _ffn(x1, g2, w_up, cw, cb, w_down, gf):
    B, S, D = x1.shape
    tm = FFN_TM
    d_ff = w_down.shape[0]
    hb = tm // SUBLANES
    n_hb = S // SUBLANES
    resident = dict(pipeline_mode=pl.Buffered(1))
    return pl.pallas_call(
        _ffn_kernel,
        grid=(B, S // tm),
        in_specs=[
            pl.BlockSpec((1, SUBLANES, D), lambda b, i: (b, jnp.maximum(i * hb - 1, 0), 0)),
            pl.BlockSpec((1, tm, D), lambda b, i: (b, i, 0)),
            pl.BlockSpec((1, SUBLANES, D), lambda b, i: (b, jnp.minimum((i + 1) * hb, n_hb - 1), 0)),
            pl.BlockSpec((1, D), lambda b, i: (0, 0)),
            pl.BlockSpec((D, d_ff), lambda b, i: (0, 0), **resident),
            pl.BlockSpec((D, d_ff), lambda b, i: (0, 1), **resident),
            pl.BlockSpec((cw.shape[0], d_ff), lambda b, i: (0, 0)),
            pl.BlockSpec((cw.shape[0], d_ff), lambda b, i: (0, 1)),
            pl.BlockSpec((1, d_ff), lambda b, i: (0, 0)),
            pl.BlockSpec((1, d_ff), lambda b, i: (0, 1)),
            pl.BlockSpec((d_ff, D), lambda b, i: (0, 0), **resident),
            pl.BlockSpec((1, D), lambda b, i: (0, 0)),
        ],
        out_specs=pl.BlockSpec((1, tm, D), lambda b, i: (b, i, 0)),
        out_shape=jax.ShapeDtypeStruct((B, S, D), F32),
        scratch_shapes=[pltpu.VMEM((D // LANES, tm, LANES), F32), pltpu.VMEM((tm, D), F32)],
        compiler_params=pltpu.CompilerParams(
            dimension_semantics=("arbitrary", "arbitrary"),
            vmem_limit_bytes=VMEM_LIMIT_BYTES),
        name="conv_ffn",
    )(x1, x1, x1, g2, w_up, w_up, cw, cw, cb, cb, w_down, gf)


def _rope_tables(S):
    t = jnp.arange(S, dtype=jnp.int32)
    pos = jnp.stack([t // GRID_W, t % GRID_W], axis=1).astype(F32)
    quarter = HEAD_DIM // 4
    inv_freq = 1.0 / (ROPE_THETA ** (jnp.arange(0, 2 * quarter, 2, dtype=F32) / (2 * quarter)))
    ang = pos[:, :, None] * inv_freq[None, None, :]
    cos = jnp.cos(ang)
    sin = jnp.sin(ang)
    zero = jnp.zeros_like(sin)
    cos_h = jnp.concatenate([cos, cos], axis=2).reshape(S, HEAD_DIM)
    sa_h = jnp.concatenate([-sin, zero], axis=2).reshape(S, HEAD_DIM)
    sb_h = jnp.concatenate([zero, sin], axis=2).reshape(S, HEAD_DIM)
    rep = LANES // HEAD_DIM
    return (jnp.tile(cos_h, (1, rep)), jnp.tile(sa_h, (1, rep)), jnp.tile(sb_h, (1, rep)))


def _block_diag_pairs(w):
    nb, bs, _ = w.shape
    w2 = w.reshape(nb // 2, 2, bs, bs)
    z = jnp.zeros((nb // 2, bs, bs), w.dtype)
    top = jnp.concatenate([w2[:, 0], z], axis=2)
    bot = jnp.concatenate([z, w2[:, 1]], axis=2)
    return jnp.concatenate([top, bot], axis=1)


def kernel(x, norm1_g, w_in, q_norm_g, k_norm_g, lru_conv_w, lru_conv_b, wa_f, ba_f, wx_f, bx_f, lam_f,
           wa_b, ba_b, wx_b, bx_b, lam_b, w_out, norm2_g, w_up, up_conv_w, up_conv_b, w_down, final_g):
    B, S, D = x.shape
    depth = w_in.shape[0]
    assert depth == 1, "the final RMSNorm is fused into the (single) layer's channel mixer"
    lru_w = lam_f.shape[1]
    n_grp = lru_w // LANES
    cos, sa, sb = _rope_tables(S)
    idx = jnp.arange(MXU_DIM) // HEAD_DIM
    e = (idx[:, None] == idx[None, :]).astype(BF16)

    for l in range(depth):
        gqk = jnp.concatenate([jnp.tile(q_norm_g[l] * (HEAD_DIM ** -0.5 * math.log2(math.e)), N_HEADS),
                               jnp.tile(k_norm_g[l], N_KV_HEADS)])[None, :]
        q, k, vt, xr, gy = _in_proj(x, norm1_g[l][None, :], w_in[l].astype(BF16), gqk, cos, sa, sb, e)
        c = (ATTN_BOUND_MARGIN * HEAD_DIM) * jnp.max(jnp.abs(gqk[:, :ATTN_WIDTH])) \
            * jnp.max(jnp.abs(gqk[:, ATTN_WIDTH:]))
        ot = _attention(q, k, vt, c.reshape(1))

        wg = (0.5 * jnp.concatenate([_block_diag_pairs(w[l]) for w in (wa_f, wx_f, wa_b, wx_b)],
                                    axis=2)).astype(BF16)
        bg = 0.5 * jnp.concatenate([b[l].reshape(n_grp, 1, LANES) for b in (ba_f, bx_f, ba_b, bx_b)],
                                   axis=2)
        lru = _lru(xr, gy, lru_conv_w[l], lru_conv_b[l][None, :], wg, bg,
                   lam_f[l][None, :], lam_b[l][None, :])

        wo = w_out[l].astype(BF16)
        x1 = _out_proj(ot, lru, x, wo[:ATTN_WIDTH], wo[ATTN_WIDTH:])
        x = _ffn(x1, norm2_g[l][None, :], w_up[l].astype(BF16), up_conv_w[l], up_conv_b[l][None, :],
                 w_down[l].astype(BF16), final_g[None, :])
    return x
```

```python
import math

import jax
import jax.numpy as jnp
from jax import lax
from jax.experimental import pallas as pl
from jax.experimental.pallas import tpu as pltpu

F32 = jnp.float32
BF16 = jnp.bfloat16

N_HEADS = 8
N_KV_HEADS = 2
HEAD_DIM = 64
GQA_GROUP = N_HEADS // N_KV_HEADS
ATTN_WIDTH = N_HEADS * HEAD_DIM
KV_WIDTH = N_KV_HEADS * HEAD_DIM
LRU_C = 8.0
GRID_W = 64
ROPE_THETA = 10000.0
EPS = 1e-6

LANES = 128
SUBLANES = 8
BF16_ROWS = 16
MXU_DIM = 256
VMEM_LIMIT_BYTES = 56 * 1024 * 1024

PROJ_TM = 512
OUT_TM = 1024
ATTN_TQ = 256
ATTN_KC = 256
ATTN_MAX_BOUND = 40.0
ATTN_BOUND_MARGIN = 1.03
DEN_ROWS = BF16_ROWS
LRU_TC = 256
FFN_TM = 256
FFN_SUB = 1024


def _gelu_tanh(x):
    return 0.5 * x * (1.0 + jnp.tanh(math.sqrt(2.0 / math.pi) * (x + 0.044715 * (x * x * x))))


def _in_proj_kernel(x_ref, g1_ref, w_ref, gqk_ref, cos_ref, sa_ref, sb_ref, e_ref,
                    q_ref, k_ref, vt_ref, xr_ref, gy_ref):
    x = x_ref[0]
    ms = jnp.mean(x * x, axis=-1, keepdims=True)
    h = (x * lax.rsqrt(ms + EPS) * g1_ref[...]).astype(BF16)

    n_qkv = ATTN_WIDTH + 2 * KV_WIDTH
    z = jnp.dot(h, w_ref[:, 0:n_qkv], preferred_element_type=F32)

    sq = (z * z).astype(BF16)
    e = e_ref[...]
    sums = [jnp.dot(sq[:, c * MXU_DIM:(c + 1) * MXU_DIM], e, preferred_element_type=F32)
            for c in range(n_qkv // MXU_DIM)]
    ssum = jnp.concatenate(sums, axis=1)
    n_qk = ATTN_WIDTH + KV_WIDTH
    qk = z[:, 0:n_qk] * lax.rsqrt(ssum[:, 0:n_qk] * (1.0 / HEAD_DIM) + EPS) * gqk_ref[...]

    cos = cos_ref[...]
    sa = sa_ref[...]
    sb = sb_ref[...]
    roped = []
    for c in range(n_qk // LANES):
        xc = qk[:, c * LANES:(c + 1) * LANES]
        up = pltpu.roll(xc, LANES - 16, 1)
        dn = pltpu.roll(xc, 16, 1)
        roped.append(xc * cos + up * sa + dn * sb)
    for c in range(ATTN_WIDTH // LANES):
        q_ref[0, c * LANES:(c + 1) * LANES, :] = roped[c].T.astype(BF16)
    k_ref[0] = roped[ATTN_WIDTH // LANES].astype(BF16)

    vt_ref[0] = z[:, n_qk:n_qkv].T.astype(BF16)

    lru_w = xr_ref.shape[2]
    xr_ref[0] = jnp.dot(h, w_ref[:, n_qkv:n_qkv + lru_w], preferred_element_type=F32)
    yr = jnp.dot(h, w_ref[:, n_qkv + lru_w:n_qkv + 2 * lru_w], preferred_element_type=F32)
    gy_ref[0] = _gelu_tanh(yr).astype(BF16)


def _in_proj(x, g1, w_in, gqk, cos, sa, sb, e):
    B, S, D = x.shape
    tm = PROJ_TM
    n_in = w_in.shape[1]
    lru_w = (n_in - ATTN_WIDTH - 2 * KV_WIDTH) // 2
    n_qk = ATTN_WIDTH + KV_WIDTH
    grid = (B, S // tm)
    tab = pl.BlockSpec((tm, LANES), lambda b, i: (i, 0))
    return pl.pallas_call(
        _in_proj_kernel,
        grid=grid,
        in_specs=[
            pl.BlockSpec((1, tm, D), lambda b, i: (b, i, 0)),
            pl.BlockSpec((1, D), lambda b, i: (0, 0)),
            pl.BlockSpec((D, n_in), lambda b, i: (0, 0)),
            pl.BlockSpec((1, n_qk), lambda b, i: (0, 0)),
            tab, tab, tab,
            pl.BlockSpec((MXU_DIM, MXU_DIM), lambda b, i: (0, 0)),
        ],
        out_specs=[
            pl.BlockSpec((1, ATTN_WIDTH, tm), lambda b, i: (b, 0, i)),
            pl.BlockSpec((1, tm, KV_WIDTH), lambda b, i: (b, i, 0)),
            pl.BlockSpec((1, KV_WIDTH, tm), lambda b, i: (b, 0, i)),
            pl.BlockSpec((1, tm, lru_w), lambda b, i: (b, i, 0)),
            pl.BlockSpec((1, tm, lru_w), lambda b, i: (b, i, 0)),
        ],
        out_shape=[
            jax.ShapeDtypeStruct((B, ATTN_WIDTH, S), BF16),
            jax.ShapeDtypeStruct((B, S, KV_WIDTH), BF16),
            jax.ShapeDtypeStruct((B, KV_WIDTH, S), BF16),
            jax.ShapeDtypeStruct((B, S, lru_w), F32),
            jax.ShapeDtypeStruct((B, S, lru_w), BF16),
        ],
        compiler_params=pltpu.CompilerParams(
            dimension_semantics=("arbitrary", "arbitrary"),
            vmem_limit_bytes=VMEM_LIMIT_BYTES),
        name="in_proj",
    )(x, g1, w_in, gqk, cos, sa, sb, e)


def _attn_kernel(qt_ref, k_ref, vt_ref, o_ref):
    tq = qt_ref.shape[2]
    S = k_ref.shape[1]
    k = k_ref[0]
    ones = jnp.ones((DEN_ROWS, S), BF16)
    zeros = jnp.zeros((HEAD_DIM, tq), BF16)
    for h in range(N_HEADS):
        j = h // GQA_GROUP
        qt = qt_ref[0, h * HEAD_DIM:(h + 1) * HEAD_DIM, :]
        qpad = jnp.concatenate([qt, zeros] if j == 0 else [zeros, qt], axis=0)
        s = jnp.dot(k, qpad, preferred_element_type=F32)
        m = jnp.max(s, axis=0, keepdims=True)
        p = jnp.exp2(s - m).astype(BF16)
        vext = jnp.concatenate([vt_ref[0, j * HEAD_DIM:(j + 1) * HEAD_DIM, :], ones], axis=0)
        pv = jnp.dot(vext, p, preferred_element_type=F32)
        o = pv[0:HEAD_DIM] / pv[HEAD_DIM:HEAD_DIM + 1]
        o_ref[0, h * HEAD_DIM:(h + 1) * HEAD_DIM, :] = o.astype(BF16)


def _attn_bounded_kernel(c_ref, qt_ref, k_ref, vt_ref, o_ref):
    tq = qt_ref.shape[2]
    S = k_ref.shape[1]
    assert tq == MXU_DIM and ATTN_KC == MXU_DIM
    c = c_ref[0]
    n_chunks = S // ATTN_KC
    n_mxu = 2
    n_rounds = N_HEADS // n_mxu
    n_items = n_rounds * n_chunks
    tile_entries = MXU_DIM // 4
    score_acc = (0, tile_entries)
    out_acc = (2 * tile_entries, 3 * tile_entries)
    quarter = ATTN_KC // 4
    ones = jnp.ones((DEN_ROWS, ATTN_KC), BF16)
    zeros_q = jnp.zeros((HEAD_DIM, tq), BF16)
    zeros_k = jnp.zeros((quarter, MXU_DIM - KV_WIDTH), BF16)

    def q_rhs(r, m):
        h = r * n_mxu + m
        rows = [zeros_q] * (MXU_DIM // HEAD_DIM)
        rows[h // GQA_GROUP] = qt_ref[0, h * HEAD_DIM:(h + 1) * HEAD_DIM, :]
        return jnp.concatenate(rows, axis=0)

    for m in range(n_mxu):
        pltpu.matmul_push_rhs(q_rhs(0, m), staging_register=0, mxu_index=m)
    p_tiles = [None] * n_mxu
    for t in range(n_items + 2):
        if t >= 2:
            for m in range(n_mxu):
                pltpu.matmul_push_rhs(p_tiles[m], staging_register=1, mxu_index=m)
        new_tiles = [[] for _ in range(n_mxu)]
        for qd in range(4):
            if t < n_items:
                ci = t % n_chunks
                rows = slice(ci * ATTN_KC + qd * quarter, ci * ATTN_KC + (qd + 1) * quarter)
                lhs = jnp.concatenate([k_ref[0, rows, :], zeros_k], axis=1)
                for m in range(n_mxu):
                    pltpu.matmul_acc_lhs(score_acc[t % 2] + qd * (quarter // 4), lhs, mxu_index=m,
                                         load_staged_rhs=0 if qd == 0 else None)
            if 1 <= t <= n_items:
                for m in range(n_mxu):
                    s = pltpu.matmul_pop(score_acc[(t - 1) % 2] + qd * (quarter // 4), (quarter, tq), F32,
                                         mxu_index=m)
                    new_tiles[m].append(jnp.exp2(s - c).astype(BF16))
        if t + 1 < n_items:
            for m in range(n_mxu):
                pltpu.matmul_push_rhs(q_rhs((t + 1) // n_chunks, m), staging_register=0, mxu_index=m)
        if t >= 2:
            r, cj = divmod(t - 2, n_chunks)
            for m in range(n_mxu):
                h = r * n_mxu + m
                j = h // GQA_GROUP
                vext = jnp.concatenate(
                    [vt_ref[0, j * HEAD_DIM:(j + 1) * HEAD_DIM, cj * ATTN_KC:(cj + 1) * ATTN_KC], ones],
                    axis=0)
                pltpu.matmul_acc_lhs(out_acc[r % 2], vext, mxu_index=m, load_staged_rhs=1)
                if cj == n_chunks - 1:
                    acc = pltpu.matmul_pop(out_acc[r % 2], (HEAD_DIM + DEN_ROWS, tq), F32, mxu_index=m)
                    o = acc[0:HEAD_DIM] / acc[HEAD_DIM:HEAD_DIM + 1]
                    o_ref[0, h * HEAD_DIM:(h + 1) * HEAD_DIM, :] = o.astype(BF16)
        if 1 <= t <= n_items:
            p_tiles = [jnp.concatenate(tl, axis=0) for tl in new_tiles]


def _attention(q, k, vt, c):
    B, _, S = q.shape
    tq = ATTN_TQ
    specs = [
        pl.BlockSpec((1, ATTN_WIDTH, tq), lambda b, i: (b, 0, i)),
        pl.BlockSpec((1, S, KV_WIDTH), lambda b, i: (b, 0, 0)),
        pl.BlockSpec((1, KV_WIDTH, S), lambda b, i: (b, 0, 0)),
    ]
    common = dict(
        grid=(B, S // tq),
        out_specs=pl.BlockSpec((1, ATTN_WIDTH, tq), lambda b, i: (b, 0, i)),
        out_shape=jax.ShapeDtypeStruct((B, ATTN_WIDTH, S), BF16),
        compiler_params=pltpu.CompilerParams(
            dimension_semantics=("arbitrary", "arbitrary"),
            vmem_limit_bytes=VMEM_LIMIT_BYTES),
    )
    bounded = pl.pallas_call(
        _attn_bounded_kernel,
        in_specs=[pl.BlockSpec(memory_space=pltpu.SMEM)] + specs,
        name="attention_bounded", **common)
    exact = pl.pallas_call(_attn_kernel, in_specs=specs, name="attention", **common)
    return lax.cond(c[0] <= ATTN_MAX_BOUND,
                    lambda: bounded(c, q, k, vt),
                    lambda: exact(q, k, vt))


def _lru_kernel(x_ref, gy_ref, cw_ref, cb_ref, wg_ref, bg_ref, lamf_ref, lamb_ref,
                o_ref, xpad, af, bf, ab, bb, hf_s, pf_s, hb_s, pb_s, cf_ref, cr_ref):
    S = x_ref.shape[1]
    W = x_ref.shape[2]
    tc = LRU_TC
    pad = SUBLANES
    seg = S // SUBLANES
    pitch = seg + SUBLANES

    xpad[0:pad, :] = jnp.zeros((pad, W), F32)
    xpad[pad + S:pad + S + pad, :] = jnp.zeros((pad, W), F32)
    xpad[pad:pad + S, :] = x_ref[0]

    def log_sigmoid(v):
        return -(jnp.maximum(-v, 0.0) + jnp.log1p(jnp.exp(-jnp.abs(v))))

    hl_f = (0.5 * LRU_C) * log_sigmoid(lamf_ref[...])
    hl_b = (0.5 * LRU_C) * log_sigmoid(lamb_ref[...])
    cw = cw_ref[...]
    cb = cb_ref[...]
    bg = bg_ref[0]
    per_seg = seg // tc
    n_chunks = S // tc

    def chunk_row(ci):
        return pl.multiple_of((ci // per_seg) * pitch + (ci % per_seg) * tc, SUBLANES)

    def gates(ci, first_row, last_row):
        t0 = pl.multiple_of(ci * tc, tc)
        xw = xpad[pl.ds(t0, tc + 2 * pad), :]
        xc = (cw[0:1] * xw[pad - 2:pad - 2 + tc] + cw[1:2] * xw[pad - 1:pad - 1 + tc]
              + cw[2:3] * xw[pad:pad + tc] + cw[3:4] * xw[pad + 1:pad + 1 + tc] + cb)
        g = jnp.dot(xc.astype(BF16), wg_ref[0], preferred_element_type=F32) + bg
        xh = 0.5 * xc
        r0 = chunk_row(ci)
        trow = lax.broadcasted_iota(jnp.int32, (tc, 1), 0)
        for d, (hl, a_s, b_s, unit) in enumerate(((hl_f, af, bf, 0 if first_row else None),
                                                  (hl_b, ab, bb, tc - 1 if last_row else None))):
            tr = jnp.tanh(g[:, (2 * d) * W:(2 * d + 1) * W])
            ti = jnp.tanh(g[:, (2 * d + 1) * W:(2 * d + 2) * W])
            log_a = tr * hl + hl
            a = jnp.exp(log_a)
            m2 = jnp.tanh(log_a) * (-1.0 - a * a)
            mult = m2 * lax.rsqrt(jnp.maximum(m2, 1e-30))
            if unit is not None:
                mult = jnp.where(trow == unit, 1.0, mult)
            a_s[pl.ds(r0, tc), :] = a
            b_s[pl.ds(r0, tc), :] = mult * (ti * xh + xh)
        return 0

    gates(0, True, False)
    lax.fori_loop(1, n_chunks - 1, lambda ci, _: gates(ci, False, False), 0, unroll=2)
    gates(n_chunks - 1, False, True)

    def step(a_s, b_s, h_s, p_s, r, h, A):
        rows = pl.ds(r, SUBLANES, stride=pitch)
        a = a_s[rows, :]
        h = a * h + b_s[rows, :]
        A = a * A
        h_s[rows, :] = h
        p_s[rows, :] = A
        return h, A

    def scan(r, carry):
        hf, Af, hb, Ab = carry
        hf, Af = step(af, bf, hf_s, pf_s, r, hf, Af)
        hb, Ab = step(ab, bb, hb_s, pb_s, seg - 1 - r, hb, Ab)
        return hf, Af, hb, Ab

    z8 = jnp.zeros((SUBLANES, W), F32)
    o8 = jnp.ones((SUBLANES, W), F32)
    hf, Af, hb, Ab = lax.fori_loop(0, seg, scan, (z8, o8, z8, o8), unroll=8)

    c = jnp.zeros((1, W), F32)
    rows = [c]
    for sgm in range(SUBLANES - 1):
        c = hf[sgm:sgm + 1] + Af[sgm:sgm + 1] * c
        rows.append(c)
    cf_ref[...] = jnp.concatenate(rows, axis=0)
    c = jnp.zeros((1, W), F32)
    rows = [c]
    for sgm in range(SUBLANES - 1, 0, -1):
        c = hb[sgm:sgm + 1] + Ab[sgm:sgm + 1] * c
        rows.append(c)
    cr_ref[...] = jnp.concatenate(rows[::-1], axis=0)

    def combine(ci, _):
        t0 = pl.multiple_of(ci * tc, tc)
        r0 = chunk_row(ci)
        sgm = ci // per_seg
        h_f = hf_s[pl.ds(r0, tc), :] + pf_s[pl.ds(r0, tc), :] * cf_ref[pl.ds(sgm, 1), :]
        h_b = hb_s[pl.ds(r0, tc), :] + pb_s[pl.ds(r0, tc), :] * cr_ref[pl.ds(sgm, 1), :]
        o_ref[0, pl.ds(t0, tc), :] = ((h_f + h_b) * gy_ref[0, pl.ds(t0, tc), :].astype(F32)).astype(BF16)
        return 0

    lax.fori_loop(0, n_chunks, combine, 0)


def _lru(xr, gy, conv_w, conv_b, wg, bg, lam_f, lam_b):
    B, S, C = xr.shape
    W = LANES
    n_grp = C // W
    seq = pl.BlockSpec((1, S, W), lambda b, c: (b, 0, c))
    vec = pl.BlockSpec((1, W), lambda b, c: (0, c))
    seg_rows = S + SUBLANES * SUBLANES
    return pl.pallas_call(
        _lru_kernel,
        grid=(B, n_grp),
        in_specs=[
            seq, seq,
            pl.BlockSpec((conv_w.shape[0], W), lambda b, c: (0, c)),
            vec,
            pl.BlockSpec((1, W, 4 * W), lambda b, c: (c, 0, 0)),
            pl.BlockSpec((1, 1, 4 * W), lambda b, c: (c, 0, 0)),
            vec, vec,
        ],
        out_specs=seq,
        out_shape=jax.ShapeDtypeStruct((B, S, C), BF16),
        scratch_shapes=[
            pltpu.VMEM((S + 2 * SUBLANES, W), F32),
            pltpu.VMEM((seg_rows, W), F32), pltpu.VMEM((seg_rows, W), F32),
            pltpu.VMEM((seg_rows, W), F32), pltpu.VMEM((seg_rows, W), F32),
            pltpu.VMEM((seg_rows, W), F32), pltpu.VMEM((seg_rows, W), F32),
            pltpu.VMEM((seg_rows, W), F32), pltpu.VMEM((seg_rows, W), F32),
            pltpu.VMEM((SUBLANES, W), F32), pltpu.VMEM((SUBLANES, W), F32),
        ],
        compiler_params=pltpu.CompilerParams(
            dimension_semantics=("arbitrary", "arbitrary"),
            vmem_limit_bytes=VMEM_LIMIT_BYTES),
        name="rg_lru",
    )(xr, gy, conv_w, conv_b, wg, bg, lam_f, lam_b)


def _out_proj_kernel(ot_ref, lru_ref, x_ref, wa_ref, wl_ref, x1_ref):
    mixed = lax.dot_general(ot_ref[0], wa_ref[...], (((0,), (0,)), ((), ())),
                            preferred_element_type=F32)
    mixed = mixed + jnp.dot(lru_ref[0], wl_ref[...], preferred_element_type=F32)
    x1_ref[0] = x_ref[0] + mixed


def _out_proj(ot, lru, x, wo):
    B, S, D = x.shape
    tm = OUT_TM
    return pl.pallas_call(
        _out_proj_kernel,
        grid=(B, S // tm),
        in_specs=[
            pl.BlockSpec((1, ATTN_WIDTH, tm), lambda b, i: (b, 0, i)),
            pl.BlockSpec((1, tm, lru.shape[2]), lambda b, i: (b, i, 0)),
            pl.BlockSpec((1, tm, D), lambda b, i: (b, i, 0)),
            pl.BlockSpec((ATTN_WIDTH, D), lambda b, i: (0, 0)),
            pl.BlockSpec((lru.shape[2], D), lambda b, i: (1, 0)),
        ],
        out_specs=pl.BlockSpec((1, tm, D), lambda b, i: (b, i, 0)),
        out_shape=jax.ShapeDtypeStruct((B, S, D), F32),
        compiler_params=pltpu.CompilerParams(
            dimension_semantics=("arbitrary", "arbitrary"),
            vmem_limit_bytes=VMEM_LIMIT_BYTES),
        name="out_proj",
    )(ot, lru, x, wo, wo)


def _ffn_kernel(xp_ref, x1_ref, xn_ref, g2_ref, wg_ref, wv_ref, cwg_ref, cwv_ref, cbg_ref, cbv_ref,
                wd_ref, gf_ref, o_ref, xs_ref, acc_ref):
    i = pl.program_id(1)
    tm = x1_ref.shape[1]
    gr = tm // SUBLANES
    n_slab = x1_ref.shape[2] // LANES

    def group(s):
        return jnp.concatenate([xs_ref[k, pl.ds(s, gr, stride=SUBLANES), :] for k in range(n_slab)], axis=1)

    def norm(xs, g):
        ms = jnp.mean(xs * xs, axis=-1, keepdims=True)
        return xs * lax.rsqrt(ms + EPS) * g

    g2 = g2_ref[...]
    for k in range(n_slab):
        xs_ref[k] = x1_ref[0, :, k * LANES:(k + 1) * LANES]
    keep_p = jnp.where(i == 0, 0.0, 1.0)
    keep_n = jnp.where(i == pl.num_programs(1) - 1, 0.0, 1.0)
    halo = jnp.concatenate([norm(xp_ref[0], g2) * keep_p, norm(xn_ref[0], g2) * keep_n], axis=0)
    h = jnp.concatenate([norm(group(s), g2).astype(BF16) for s in range(SUBLANES)]
                        + [halo.astype(BF16)], axis=0)
    n_sub = wg_ref.shape[1] // FFN_SUB

    def up(c):
        cols = slice(c * FFN_SUB, (c + 1) * FFN_SUB)
        return (jnp.dot(h, wg_ref[:, cols], preferred_element_type=F32),
                jnp.dot(h, wv_ref[:, cols], preferred_element_type=F32))

    def conv(u, cw, cb):
        grp = [u[s * gr:(s + 1) * gr] for s in range(SUBLANES)]
        t_prev = u[tm + SUBLANES - 1:tm + SUBLANES]
        t_next = u[tm + SUBLANES:tm + SUBLANES + 1]
        before = jnp.concatenate([t_prev, grp[SUBLANES - 1][:gr - 1]], axis=0)
        after = jnp.concatenate([grp[0][1:], t_next], axis=0)
        outs = []
        for s in range(SUBLANES):
            prv = before if s == 0 else grp[s - 1]
            nxt = after if s == SUBLANES - 1 else grp[s + 1]
            outs.append(cw[0:1] * prv + cw[1:2] * grp[s] + cw[2:3] * nxt + cb)
        return outs

    u_next = up(0)
    for c in range(n_sub):
        ug, uv = u_next
        if c + 1 < n_sub:
            u_next = up(c + 1)
        cols = slice(c * FFN_SUB, (c + 1) * FFN_SUB)
        gate = conv(ug, cwg_ref[:, cols], cbg_ref[:, cols])
        val = conv(uv, cwv_ref[:, cols], cbv_ref[:, cols])
        act = jnp.concatenate([(_gelu_tanh(g) * v).astype(BF16) for g, v in zip(gate, val)], axis=0)
        d = jnp.dot(act, wd_ref[c * FFN_SUB:(c + 1) * FFN_SUB, :], preferred_element_type=F32)
        if c == 0:
            acc_ref[...] = d
        else:
            acc_ref[...] += d

    gf = gf_ref[...]
    for s in range(SUBLANES):
        y = norm(group(s) + acc_ref[s * gr:(s + 1) * gr, :], gf)
        for k in range(n_slab):
            xs_ref[k, pl.ds(s, gr, stride=SUBLANES), :] = y[:, k * LANES:(k + 1) * LANES]
    for k in range(n_slab):
        o_ref[0, :, k * LANES:(k + 1) * LANES] = xs_ref[k]


def _ffn(x1, g2, w_up, cw, cb, w_down, gf):
    B, S, D = x1.shape
    tm = FFN_TM
    d_ff = w_down.shape[0]
    hb = tm // SUBLANES
    n_hb = S // SUBLANES
    resident = dict(pipeline_mode=pl.Buffered(1))
    return pl.pallas_call(
        _ffn_kernel,
        grid=(B, S // tm),
        in_specs=[
            pl.BlockSpec((1, SUBLANES, D), lambda b, i: (b, jnp.maximum(i * hb - 1, 0), 0)),
            pl.BlockSpec((1, tm, D), lambda b, i: (b, i, 0)),
            pl.BlockSpec((1, SUBLANES, D), lambda b, i: (b, jnp.minimum((i + 1) * hb, n_hb - 1), 0)),
            pl.BlockSpec((1, D), lambda b, i: (0, 0)),
            pl.BlockSpec((D, d_ff), lambda b, i: (0, 0), **resident),
            pl.BlockSpec((D, d_ff), lambda b, i: (0, 1), **resident),
            pl.BlockSpec((cw.shape[0], d_ff), lambda b, i: (0, 0)),
            pl.BlockSpec((cw.shape[0], d_ff), lambda b, i: (0, 1)),
            pl.BlockSpec((1, d_ff), lambda b, i: (0, 0)),
            pl.BlockSpec((1, d_ff), lambda b, i: (0, 1)),
            pl.BlockSpec((d_ff, D), lambda b, i: (0, 0), **resident),
            pl.BlockSpec((1, D), lambda b, i: (0, 0)),
        ],
        out_specs=pl.BlockSpec((1, tm, D), lambda b, i: (b, i, 0)),
        out_shape=jax.ShapeDtypeStruct((B, S, D), F32),
        scratch_shapes=[pltpu.VMEM((D // LANES, tm, LANES), F32), pltpu.VMEM((tm, D), F32)],
        compiler_params=pltpu.CompilerParams(
            dimension_semantics=("arbitrary", "arbitrary"),
            vmem_limit_bytes=VMEM_LIMIT_BYTES),
        name="conv_ffn",
    )(x1, x1, x1, g2, w_up, w_up, cw, cw, cb, cb, w_down, gf)


def _rope_tables(S):
    t = jnp.arange(S, dtype=jnp.int32)
    pos = jnp.stack([t // GRID_W, t % GRID_W], axis=1).astype(F32)
    quarter = HEAD_DIM // 4
    inv_freq = 1.0 / (ROPE_THETA ** (jnp.arange(0, 2 * quarter, 2, dtype=F32) / (2 * quarter)))
    ang = pos[:, :, None] * inv_freq[None, None, :]
    cos = jnp.cos(ang)
    sin = jnp.sin(ang)
    zero = jnp.zeros_like(sin)
    cos_h = jnp.concatenate([cos, cos], axis=2).reshape(S, HEAD_DIM)
    sa_h = jnp.concatenate([-sin, zero], axis=2).reshape(S, HEAD_DIM)
    sb_h = jnp.concatenate([zero, sin], axis=2).reshape(S, HEAD_DIM)
    rep = LANES // HEAD_DIM
    return (jnp.tile(cos_h, (1, rep)), jnp.tile(sa_h, (1, rep)), jnp.tile(sb_h, (1, rep)))


def _block_diag_pairs(w):
    nb, bs, _ = w.shape
    w2 = w.reshape(nb // 2, 2, bs, bs)
    z = jnp.zeros((nb // 2, bs, bs), w.dtype)
    top = jnp.concatenate([w2[:, 0], z], axis=2)
    bot = jnp.concatenate([z, w2[:, 1]], axis=2)
    return jnp.concatenate([top, bot], axis=1)


def kernel(x, norm1_g, w_in, q_norm_g, k_norm_g, lru_conv_w, lru_conv_b, wa_f, ba_f, wx_f, bx_f, lam_f,
           wa_b, ba_b, wx_b, bx_b, lam_b, w_out, norm2_g, w_up, up_conv_w, up_conv_b, w_down, final_g):
    B, S, D = x.shape
    depth = w_in.shape[0]
    assert depth == 1, "the final RMSNorm is fused into the (single) layer's channel mixer"
    lru_w = lam_f.shape[1]
    n_grp = lru_w // LANES
    cos, sa, sb = _rope_tables(S)
    idx = jnp.arange(MXU_DIM) // HEAD_DIM
    e = (idx[:, None] == idx[None, :]).astype(BF16)

    for l in range(depth):
        gqk = jnp.concatenate([jnp.tile(q_norm_g[l] * (HEAD_DIM ** -0.5 * math.log2(math.e)), N_HEADS),
                               jnp.tile(k_norm_g[l], N_KV_HEADS)])[None, :]
        q, k, vt, xr, gy = _in_proj(x, norm1_g[l][None, :], w_in[l].astype(BF16), gqk, cos, sa, sb, e)
        c = (ATTN_BOUND_MARGIN * HEAD_DIM) * jnp.max(jnp.abs(gqk[:, :ATTN_WIDTH])) \
            * jnp.max(jnp.abs(gqk[:, ATTN_WIDTH:]))
        ot = _attention(q, k, vt, c.reshape(1))

        wg = (0.5 * jnp.concatenate([_block_diag_pairs(w[l]) for w in (wa_f, wx_f, wa_b, wx_b)],
                                    axis=2)).astype(BF16)
        bg = 0.5 * jnp.concatenate([b[l].reshape(n_grp, 1, LANES) for b in (ba_f, bx_f, ba_b, bx_b)],
                                   axis=2)
        lru = _lru(xr, gy, lru_conv_w[l], lru_conv_b[l][None, :], wg, bg,
                   lam_f[l][None, :], lam_b[l][None, :])

        assert lru_w == ATTN_WIDTH, "w_out is fetched as two equal row blocks"
        x1 = _out_proj(ot, lru, x, w_out[l].astype(BF16))
        x = _ffn(x1, norm2_g[l][None, :], w_up[l].astype(BF16), up_conv_w[l], up_conv_b[l][None, :],
                 w_down[l].astype(BF16), final_g[None, :])
    return x
```

```python
import math

import jax
import jax.numpy as jnp
from jax import lax
from jax.experimental import pallas as pl
from jax.experimental.pallas import tpu as pltpu

F32 = jnp.float32
BF16 = jnp.bfloat16

N_HEADS = 8
N_KV_HEADS = 2
HEAD_DIM = 64
GQA_GROUP = N_HEADS // N_KV_HEADS
ATTN_WIDTH = N_HEADS * HEAD_DIM
KV_WIDTH = N_KV_HEADS * HEAD_DIM
LRU_C = 8.0
GRID_W = 64
ROPE_THETA = 10000.0
EPS = 1e-6

LANES = 128
SUBLANES = 8
BF16_ROWS = 16
MXU_DIM = 256
VMEM_LIMIT_BYTES = 56 * 1024 * 1024

PROJ_TM = 1024
OUT_TM = 1024
ATTN_TQ = 256
ATTN_KC = 256
ATTN_MAX_BOUND = 40.0
ATTN_BOUND_MARGIN = 1.03
DEN_ROWS = BF16_ROWS
LRU_TC = 256
FFN_TM = 256
FFN_SUB = 1024


def _gelu_tanh(x):
    return 0.5 * x * (1.0 + jnp.tanh(math.sqrt(2.0 / math.pi) * (x + 0.044715 * (x * x * x))))


def _in_proj_kernel(x_ref, g1_ref, w_ref, gqk_ref, cos_ref, sa_ref, sb_ref, e_ref,
                    q_ref, k_ref, vt_ref, xr_ref, gy_ref):
    x = x_ref[0]
    ms = jnp.mean(x * x, axis=-1, keepdims=True)
    h = (x * lax.rsqrt(ms + EPS) * g1_ref[...]).astype(BF16)

    n_qkv = ATTN_WIDTH + 2 * KV_WIDTH
    z = jnp.dot(h, w_ref[:, 0:n_qkv], preferred_element_type=F32)

    sq = (z * z).astype(BF16)
    e = e_ref[...]
    sums = [jnp.dot(sq[:, c * MXU_DIM:(c + 1) * MXU_DIM], e, preferred_element_type=F32)
            for c in range(n_qkv // MXU_DIM)]
    ssum = jnp.concatenate(sums, axis=1)
    n_qk = ATTN_WIDTH + KV_WIDTH
    qk = z[:, 0:n_qk] * lax.rsqrt(ssum[:, 0:n_qk] * (1.0 / HEAD_DIM) + EPS) * gqk_ref[...]

    cos = cos_ref[...]
    sa = sa_ref[...]
    sb = sb_ref[...]
    roped = []
    for c in range(n_qk // LANES):
        xc = qk[:, c * LANES:(c + 1) * LANES]
        up = pltpu.roll(xc, LANES - 16, 1)
        dn = pltpu.roll(xc, 16, 1)
        roped.append(xc * cos + up * sa + dn * sb)
    for c in range(ATTN_WIDTH // LANES):
        q_ref[0, c * LANES:(c + 1) * LANES, :] = roped[c].T.astype(BF16)
    k_ref[0] = roped[ATTN_WIDTH // LANES].astype(BF16)

    vt_ref[0] = z[:, n_qk:n_qkv].T.astype(BF16)

    lru_w = xr_ref.shape[2]
    xr_ref[0] = jnp.dot(h, w_ref[:, n_qkv:n_qkv + lru_w], preferred_element_type=F32)
    yr = jnp.dot(h, w_ref[:, n_qkv + lru_w:n_qkv + 2 * lru_w], preferred_element_type=F32)
    gy_ref[0] = _gelu_tanh(yr).astype(BF16)


def _in_proj(x, g1, w_in, gqk, cos, sa, sb, e):
    B, S, D = x.shape
    tm = PROJ_TM
    n_in = w_in.shape[1]
    lru_w = (n_in - ATTN_WIDTH - 2 * KV_WIDTH) // 2
    n_qk = ATTN_WIDTH + KV_WIDTH
    grid = (B, S // tm)
    tab = pl.BlockSpec((tm, LANES), lambda b, i: (i, 0))
    return pl.pallas_call(
        _in_proj_kernel,
        grid=grid,
        in_specs=[
            pl.BlockSpec((1, tm, D), lambda b, i: (b, i, 0)),
            pl.BlockSpec((1, D), lambda b, i: (0, 0)),
            pl.BlockSpec((D, n_in), lambda b, i: (0, 0)),
            pl.BlockSpec((1, n_qk), lambda b, i: (0, 0)),
            tab, tab, tab,
            pl.BlockSpec((MXU_DIM, MXU_DIM), lambda b, i: (0, 0)),
        ],
        out_specs=[
            pl.BlockSpec((1, ATTN_WIDTH, tm), lambda b, i: (b, 0, i)),
            pl.BlockSpec((1, tm, KV_WIDTH), lambda b, i: (b, i, 0)),
            pl.BlockSpec((1, KV_WIDTH, tm), lambda b, i: (b, 0, i)),
            pl.BlockSpec((1, tm, lru_w), lambda b, i: (b, i, 0)),
            pl.BlockSpec((1, tm, lru_w), lambda b, i: (b, i, 0)),
        ],
        out_shape=[
            jax.ShapeDtypeStruct((B, ATTN_WIDTH, S), BF16),
            jax.ShapeDtypeStruct((B, S, KV_WIDTH), BF16),
            jax.ShapeDtypeStruct((B, KV_WIDTH, S), BF16),
            jax.ShapeDtypeStruct((B, S, lru_w), F32),
            jax.ShapeDtypeStruct((B, S, lru_w), BF16),
        ],
        compiler_params=pltpu.CompilerParams(
            dimension_semantics=("arbitrary", "arbitrary"),
            vmem_limit_bytes=VMEM_LIMIT_BYTES),
        name="in_proj",
    )(x, g1, w_in, gqk, cos, sa, sb, e)


def _attn_kernel(qt_ref, k_ref, vt_ref, o_ref):
    tq = qt_ref.shape[2]
    S = k_ref.shape[1]
    k = k_ref[0]
    ones = jnp.ones((DEN_ROWS, S), BF16)
    zeros = jnp.zeros((HEAD_DIM, tq), BF16)
    for h in range(N_HEADS):
        j = h // GQA_GROUP
        qt = qt_ref[0, h * HEAD_DIM:(h + 1) * HEAD_DIM, :]
        qpad = jnp.concatenate([qt, zeros] if j == 0 else [zeros, qt], axis=0)
        s = jnp.dot(k, qpad, preferred_element_type=F32)
        m = jnp.max(s, axis=0, keepdims=True)
        p = jnp.exp2(s - m).astype(BF16)
        vext = jnp.concatenate([vt_ref[0, j * HEAD_DIM:(j + 1) * HEAD_DIM, :], ones], axis=0)
        pv = jnp.dot(vext, p, preferred_element_type=F32)
        o = pv[0:HEAD_DIM] / pv[HEAD_DIM:HEAD_DIM + 1]
        o_ref[0, h * HEAD_DIM:(h + 1) * HEAD_DIM, :] = o.astype(BF16)


def _attn_bounded_kernel(c_ref, qt_ref, k_ref, vt_ref, o_ref):
    tq = qt_ref.shape[2]
    S = k_ref.shape[1]
    assert tq == MXU_DIM and ATTN_KC == MXU_DIM
    c = c_ref[0]
    n_chunks = S // ATTN_KC
    n_mxu = 2
    n_rounds = N_HEADS // n_mxu
    n_items = n_rounds * n_chunks
    tile_entries = MXU_DIM // 4
    score_acc = (0, tile_entries)
    out_acc = (2 * tile_entries, 3 * tile_entries)
    quarter = ATTN_KC // 4
    zeros_q = jnp.zeros((HEAD_DIM, tq), BF16)
    zeros_k = jnp.zeros((quarter, MXU_DIM - KV_WIDTH), BF16)

    def q_rhs(r, m):
        h = r * n_mxu + m
        rows = [zeros_q] * (MXU_DIM // HEAD_DIM)
        rows[h // GQA_GROUP] = qt_ref[0, h * HEAD_DIM:(h + 1) * HEAD_DIM, :]
        return jnp.concatenate(rows, axis=0)

    for m in range(n_mxu):
        pltpu.matmul_push_rhs(q_rhs(0, m), staging_register=0, mxu_index=m)
    p_tiles = [None] * n_mxu
    den = [jnp.zeros((SUBLANES, tq), F32) for _ in range(n_mxu)]
    for t in range(n_items + 2):
        if t >= 2:
            for m in range(n_mxu):
                pltpu.matmul_push_rhs(p_tiles[m], staging_register=1, mxu_index=m)
        new_tiles = [[] for _ in range(n_mxu)]
        for qd in range(4):
            if t < n_items:
                ci = t % n_chunks
                rows = slice(ci * ATTN_KC + qd * quarter, ci * ATTN_KC + (qd + 1) * quarter)
                lhs = jnp.concatenate([k_ref[0, rows, :], zeros_k], axis=1)
                for m in range(n_mxu):
                    pltpu.matmul_acc_lhs(score_acc[t % 2] + qd * (quarter // 4), lhs, mxu_index=m,
                                         load_staged_rhs=0 if qd == 0 else None)
            if 1 <= t <= n_items:
                for m in range(n_mxu):
                    s = pltpu.matmul_pop(score_acc[(t - 1) % 2] + qd * (quarter // 4), (quarter, tq), F32,
                                         mxu_index=m)
                    p = jnp.exp2(s - c)
                    den[m] = den[m] + p.reshape(quarter // SUBLANES, SUBLANES, tq).sum(axis=0)
                    new_tiles[m].append(p.astype(BF16))
        if t + 1 < n_items:
            for m in range(n_mxu):
                pltpu.matmul_push_rhs(q_rhs((t + 1) // n_chunks, m), staging_register=0, mxu_index=m)
        if t >= 2:
            r, cj = divmod(t - 2, n_chunks)
            for m in range(n_mxu):
                h = r * n_mxu + m
                j = h // GQA_GROUP
                vchunk = vt_ref[0, j * HEAD_DIM:(j + 1) * HEAD_DIM, cj * ATTN_KC:(cj + 1) * ATTN_KC]
                pltpu.matmul_acc_lhs(out_acc[r % 2], vchunk, mxu_index=m, load_staged_rhs=1)
                if cj == n_chunks - 1:
                    acc = pltpu.matmul_pop(out_acc[r % 2], (HEAD_DIM, tq), F32, mxu_index=m)
                    o = acc / den_done[m]
                    o_ref[0, h * HEAD_DIM:(h + 1) * HEAD_DIM, :] = o.astype(BF16)
        if 1 <= t <= n_items:
            p_tiles = [jnp.concatenate(tl, axis=0) for tl in new_tiles]
            if (t - 1) % n_chunks == n_chunks - 1:
                den_done = [jnp.sum(d, axis=0, keepdims=True) for d in den]
                den = [jnp.zeros((SUBLANES, tq), F32) for _ in range(n_mxu)]


def _attention(q, k, vt, c):
    B, _, S = q.shape
    tq = ATTN_TQ
    specs = [
        pl.BlockSpec((1, ATTN_WIDTH, tq), lambda b, i: (b, 0, i)),
        pl.BlockSpec((1, S, KV_WIDTH), lambda b, i: (b, 0, 0)),
        pl.BlockSpec((1, KV_WIDTH, S), lambda b, i: (b, 0, 0)),
    ]
    common = dict(
        grid=(B, S // tq),
        out_specs=pl.BlockSpec((1, ATTN_WIDTH, tq), lambda b, i: (b, 0, i)),
        out_shape=jax.ShapeDtypeStruct((B, ATTN_WIDTH, S), BF16),
        compiler_params=pltpu.CompilerParams(
            dimension_semantics=("arbitrary", "arbitrary"),
            vmem_limit_bytes=VMEM_LIMIT_BYTES),
    )
    bounded = pl.pallas_call(
        _attn_bounded_kernel,
        in_specs=[pl.BlockSpec(memory_space=pltpu.SMEM)] + specs,
        name="attention_bounded", **common)
    exact = pl.pallas_call(_attn_kernel, in_specs=specs, name="attention", **common)
    return lax.cond(c[0] <= ATTN_MAX_BOUND,
                    lambda: bounded(c, q, k, vt),
                    lambda: exact(q, k, vt))


def _lru_kernel(x_ref, gy_ref, cw_ref, cb_ref, wg_ref, bg_ref, lamf_ref, lamb_ref,
                o_ref, xpad, af, bf, ab, bb, hf_s, pf_s, hb_s, pb_s, cf_ref, cr_ref):
    S = x_ref.shape[1]
    W = x_ref.shape[2]
    tc = LRU_TC
    pad = SUBLANES
    seg = S // SUBLANES
    pitch = seg + SUBLANES

    xpad[0:pad, :] = jnp.zeros((pad, W), F32)
    xpad[pad + S:pad + S + pad, :] = jnp.zeros((pad, W), F32)
    xpad[pad:pad + S, :] = x_ref[0]

    def log_sigmoid(v):
        return -(jnp.maximum(-v, 0.0) + jnp.log1p(jnp.exp(-jnp.abs(v))))

    hl_f = (0.5 * LRU_C) * log_sigmoid(lamf_ref[...])
    hl_b = (0.5 * LRU_C) * log_sigmoid(lamb_ref[...])
    cw = cw_ref[...]
    cb = cb_ref[...]
    bg = bg_ref[0]
    per_seg = seg // tc
    n_chunks = S // tc

    def chunk_row(ci):
        return pl.multiple_of((ci // per_seg) * pitch + (ci % per_seg) * tc, SUBLANES)

    def gates(ci, first_row, last_row):
        t0 = pl.multiple_of(ci * tc, tc)
        xw = xpad[pl.ds(t0, tc + 2 * pad), :]
        xc = (cw[0:1] * xw[pad - 2:pad - 2 + tc] + cw[1:2] * xw[pad - 1:pad - 1 + tc]
              + cw[2:3] * xw[pad:pad + tc] + cw[3:4] * xw[pad + 1:pad + 1 + tc] + cb)
        g = jnp.dot(xc.astype(BF16), wg_ref[0], preferred_element_type=F32) + bg
        xh = 0.5 * xc
        r0 = chunk_row(ci)
        trow = lax.broadcasted_iota(jnp.int32, (tc, 1), 0)
        for d, (hl, a_s, b_s, unit) in enumerate(((hl_f, af, bf, 0 if first_row else None),
                                                  (hl_b, ab, bb, tc - 1 if last_row else None))):
            tr = jnp.tanh(g[:, (2 * d) * W:(2 * d + 1) * W])
            ti = jnp.tanh(g[:, (2 * d + 1) * W:(2 * d + 2) * W])
            log_a = tr * hl + hl
            a = jnp.exp(log_a)
            m2 = jnp.tanh(log_a) * (-1.0 - a * a)
            mult = m2 * lax.rsqrt(jnp.maximum(m2, 1e-30))
            if unit is not None:
                mult = jnp.where(trow == unit, 1.0, mult)
            a_s[pl.ds(r0, tc), :] = a
            b_s[pl.ds(r0, tc), :] = mult * (ti * xh + xh)
        return 0

    gates(0, True, False)
    lax.fori_loop(1, n_chunks - 1, lambda ci, _: gates(ci, False, False), 0, unroll=2)
    gates(n_chunks - 1, False, True)

    def step(a_s, b_s, h_s, p_s, r, h, A):
        rows = pl.ds(r, SUBLANES, stride=pitch)
        a = a_s[rows, :]
        h = a * h + b_s[rows, :]
        A = a * A
        h_s[rows, :] = h
        p_s[rows, :] = A
        return h, A

    def scan(r, carry):
        hf, Af, hb, Ab = carry
        hf, Af = step(af, bf, hf_s, pf_s, r, hf, Af)
        hb, Ab = step(ab, bb, hb_s, pb_s, seg - 1 - r, hb, Ab)
        return hf, Af, hb, Ab

    z8 = jnp.zeros((SUBLANES, W), F32)
    o8 = jnp.ones((SUBLANES, W), F32)
    hf, Af, hb, Ab = lax.fori_loop(0, seg, scan, (z8, o8, z8, o8), unroll=8)

    c = jnp.zeros((1, W), F32)
    rows = [c]
    for sgm in range(SUBLANES - 1):
        c = hf[sgm:sgm + 1] + Af[sgm:sgm + 1] * c
        rows.append(c)
    cf_ref[...] = jnp.concatenate(rows, axis=0)
    c = jnp.zeros((1, W), F32)
    rows = [c]
    for sgm in range(SUBLANES - 1, 0, -1):
        c = hb[sgm:sgm + 1] + Ab[sgm:sgm + 1] * c
        rows.append(c)
    cr_ref[...] = jnp.concatenate(rows[::-1], axis=0)

    def combine(ci, _):
        t0 = pl.multiple_of(ci * tc, tc)
        r0 = chunk_row(ci)
        sgm = ci // per_seg
        h_f = hf_s[pl.ds(r0, tc), :] + pf_s[pl.ds(r0, tc), :] * cf_ref[pl.ds(sgm, 1), :]
        h_b = hb_s[pl.ds(r0, tc), :] + pb_s[pl.ds(r0, tc), :] * cr_ref[pl.ds(sgm, 1), :]
        o_ref[0, pl.ds(t0, tc), :] = ((h_f + h_b) * gy_ref[0, pl.ds(t0, tc), :].astype(F32)).astype(BF16)
        return 0

    lax.fori_loop(0, n_chunks, combine, 0)


def _lru(xr, gy, conv_w, conv_b, wg, bg, lam_f, lam_b):
    B, S, C = xr.shape
    W = LANES
    n_grp = C // W
    seq = pl.BlockSpec((1, S, W), lambda b, c: (b, 0, c))
    vec = pl.BlockSpec((1, W), lambda b, c: (0, c))
    seg_rows = S + SUBLANES * SUBLANES
    return pl.pallas_call(
        _lru_kernel,
        grid=(B, n_grp),
        in_specs=[
            seq, seq,
            pl.BlockSpec((conv_w.shape[0], W), lambda b, c: (0, c)),
            vec,
            pl.BlockSpec((1, W, 4 * W), lambda b, c: (c, 0, 0)),
            pl.BlockSpec((1, 1, 4 * W), lambda b, c: (c, 0, 0)),
            vec, vec,
        ],
        out_specs=seq,
        out_shape=jax.ShapeDtypeStruct((B, S, C), BF16),
        scratch_shapes=[
            pltpu.VMEM((S + 2 * SUBLANES, W), F32),
            pltpu.VMEM((seg_rows, W), F32), pltpu.VMEM((seg_rows, W), F32),
            pltpu.VMEM((seg_rows, W), F32), pltpu.VMEM((seg_rows, W), F32),
            pltpu.VMEM((seg_rows, W), F32), pltpu.VMEM((seg_rows, W), F32),
            pltpu.VMEM((seg_rows, W), F32), pltpu.VMEM((seg_rows, W), F32),
            pltpu.VMEM((SUBLANES, W), F32), pltpu.VMEM((SUBLANES, W), F32),
        ],
        compiler_params=pltpu.CompilerParams(
            dimension_semantics=("arbitrary", "arbitrary"),
            vmem_limit_bytes=VMEM_LIMIT_BYTES),
        name="rg_lru",
    )(xr, gy, conv_w, conv_b, wg, bg, lam_f, lam_b)


def _out_proj_kernel(ot_ref, lru_ref, x_ref, wa_ref, wl_ref, x1_ref):
    mixed = lax.dot_general(ot_ref[0], wa_ref[...], (((0,), (0,)), ((), ())),
                            preferred_element_type=F32)
    mixed = mixed + jnp.dot(lru_ref[0], wl_ref[...], preferred_element_type=F32)
    x1_ref[0] = x_ref[0] + mixed


def _out_proj(ot, lru, x, wo):
    B, S, D = x.shape
    tm = OUT_TM
    return pl.pallas_call(
        _out_proj_kernel,
        grid=(B, S // tm),
        in_specs=[
            pl.BlockSpec((1, ATTN_WIDTH, tm), lambda b, i: (b, 0, i)),
            pl.BlockSpec((1, tm, lru.shape[2]), lambda b, i: (b, i, 0)),
            pl.BlockSpec((1, tm, D), lambda b, i: (b, i, 0)),
            pl.BlockSpec((ATTN_WIDTH, D), lambda b, i: (0, 0)),
            pl.BlockSpec((lru.shape[2], D), lambda b, i: (1, 0)),
        ],
        out_specs=pl.BlockSpec((1, tm, D), lambda b, i: (b, i, 0)),
        out_shape=jax.ShapeDtypeStruct((B, S, D), F32),
        compiler_params=pltpu.CompilerParams(
            dimension_semantics=("arbitrary", "arbitrary"),
            vmem_limit_bytes=VMEM_LIMIT_BYTES),
        name="out_proj",
    )(ot, lru, x, wo, wo)


def _ffn_kernel(xp_ref, x1_ref, xn_ref, g2_ref, wg_ref, wv_ref, cwg_ref, cwv_ref, cbg_ref, cbv_ref,
                wd_ref, gf_ref, o_ref, xs_ref, acc_ref):
    i = pl.program_id(1)
    tm = x1_ref.shape[1]
    gr = tm // SUBLANES
    n_slab = x1_ref.shape[2] // LANES

    def group(s):
        return jnp.concatenate([xs_ref[k, pl.ds(s, gr, stride=SUBLANES), :] for k in range(n_slab)], axis=1)

    def norm(xs, g):
        ms = jnp.mean(xs * xs, axis=-1, keepdims=True)
        return xs * lax.rsqrt(ms + EPS) * g

    g2 = g2_ref[...]
    for k in range(n_slab):
        xs_ref[k] = x1_ref[0, :, k * LANES:(k + 1) * LANES]
    keep_p = jnp.where(i == 0, 0.0, 1.0)
    keep_n = jnp.where(i == pl.num_programs(1) - 1, 0.0, 1.0)
    halo = jnp.concatenate([norm(xp_ref[0], g2) * keep_p, norm(xn_ref[0], g2) * keep_n], axis=0)
    h = jnp.concatenate([norm(group(s), g2).astype(BF16) for s in range(SUBLANES)]
                        + [halo.astype(BF16)], axis=0)
    n_sub = wg_ref.shape[1] // FFN_SUB

    def up(c):
        cols = slice(c * FFN_SUB, (c + 1) * FFN_SUB)
        return (jnp.dot(h, wg_ref[:, cols], preferred_element_type=F32),
                jnp.dot(h, wv_ref[:, cols], preferred_element_type=F32))

    def conv(u, cw, cb):
        grp = [u[s * gr:(s + 1) * gr] for s in range(SUBLANES)]
        t_prev = u[tm + SUBLANES - 1:tm + SUBLANES]
        t_next = u[tm + SUBLANES:tm + SUBLANES + 1]
        before = jnp.concatenate([t_prev, grp[SUBLANES - 1][:gr - 1]], axis=0)
        after = jnp.concatenate([grp[0][1:], t_next], axis=0)
        outs = []
        for s in range(SUBLANES):
            prv = before if s == 0 else grp[s - 1]
            nxt = after if s == SUBLANES - 1 else grp[s + 1]
            outs.append(cw[0:1] * prv + cw[1:2] * grp[s] + cw[2:3] * nxt + cb)
        return outs

    u_next = up(0)
    for c in range(n_sub):
        ug, uv = u_next
        if c + 1 < n_sub:
            u_next = up(c + 1)
        cols = slice(c * FFN_SUB, (c + 1) * FFN_SUB)
        gate = conv(ug, cwg_ref[:, cols], cbg_ref[:, cols])
        val = conv(uv, cwv_ref[:, cols], cbv_ref[:, cols])
        act = jnp.concatenate([(_gelu_tanh(g) * v).astype(BF16) for g, v in zip(gate, val)], axis=0)
        d = jnp.dot(act, wd_ref[c * FFN_SUB:(c + 1) * FFN_SUB, :], preferred_element_type=F32)
        if c == 0:
            acc_ref[...] = d
        else:
            acc_ref[...] += d

    gf = gf_ref[...]
    for s in range(SUBLANES):
        y = norm(group(s) + acc_ref[s * gr:(s + 1) * gr, :], gf)
        for k in range(n_slab):
            xs_ref[k, pl.ds(s, gr, stride=SUBLANES), :] = y[:, k * LANES:(k + 1) * LANES]
    for k in range(n_slab):
        o_ref[0, :, k * LANES:(k + 1) * LANES] = xs_ref[k]


def _ffn(x1, g2, w_up, cw, cb, w_down, gf):
    B, S, D = x1.shape
    tm = FFN_TM
    d_ff = w_down.shape[0]
    hb = tm // SUBLANES
    n_hb = S // SUBLANES
    resident = dict(pipeline_mode=pl.Buffered(1))
    return pl.pallas_call(
        _ffn_kernel,
        grid=(B, S // tm),
        in_specs=[
            pl.BlockSpec((1, SUBLANES, D), lambda b, i: (b, jnp.maximum(i * hb - 1, 0), 0)),
            pl.BlockSpec((1, tm, D), lambda b, i: (b, i, 0)),
            pl.BlockSpec((1, SUBLANES, D), lambda b, i: (b, jnp.minimum((i + 1) * hb, n_hb - 1), 0)),
            pl.BlockSpec((1, D), lambda b, i: (0, 0)),
            pl.BlockSpec((D, d_ff), lambda b, i: (0, 0), **resident),
            pl.BlockSpec((D, d_ff), lambda b, i: (0, 1), **resident),
            pl.BlockSpec((cw.shape[0], d_ff), lambda b, i: (0, 0)),
            pl.BlockSpec((cw.shape[0], d_ff), lambda b, i: (0, 1)),
            pl.BlockSpec((1, d_ff), lambda b, i: (0, 0)),
            pl.BlockSpec((1, d_ff), lambda b, i: (0, 1)),
            pl.BlockSpec((d_ff, D), lambda b, i: (0, 0), **resident),
            pl.BlockSpec((1, D), lambda b, i: (0, 0)),
        ],
        out_specs=pl.BlockSpec((1, tm, D), lambda b, i: (b, i, 0)),
        out_shape=jax.ShapeDtypeStruct((B, S, D), F32),
        scratch_shapes=[pltpu.VMEM((D // LANES, tm, LANES), F32), pltpu.VMEM((tm, D), F32)],
        compiler_params=pltpu.CompilerParams(
            dimension_semantics=("arbitrary", "arbitrary"),
            vmem_limit_bytes=VMEM_LIMIT_BYTES),
        name="conv_ffn",
    )(x1, x1, x1, g2, w_up, w_up, cw, cw, cb, cb, w_down, gf)


def _rope_tables(S):
    t = jnp.arange(S, dtype=jnp.int32)
    pos = jnp.stack([t // GRID_W, t % GRID_W], axis=1).astype(F32)
    quarter = HEAD_DIM // 4
    inv_freq = 1.0 / (ROPE_THETA ** (jnp.arange(0, 2 * quarter, 2, dtype=F32) / (2 * quarter)))
    ang = pos[:, :, None] * inv_freq[None, None, :]
    cos = jnp.cos(ang)
    sin = jnp.sin(ang)
    zero = jnp.zeros_like(sin)
    cos_h = jnp.concatenate([cos, cos], axis=2).reshape(S, HEAD_DIM)
    sa_h = jnp.concatenate([-sin, zero], axis=2).reshape(S, HEAD_DIM)
    sb_h = jnp.concatenate([zero, sin], axis=2).reshape(S, HEAD_DIM)
    rep = LANES // HEAD_DIM
    return (jnp.tile(cos_h, (1, rep)), jnp.tile(sa_h, (1, rep)), jnp.tile(sb_h, (1, rep)))


def _block_diag_pairs(w):
    nb, bs, _ = w.shape
    w2 = w.reshape(nb // 2, 2, bs, bs)
    z = jnp.zeros((nb // 2, bs, bs), w.dtype)
    top = jnp.concatenate([w2[:, 0], z], axis=2)
    bot = jnp.concatenate([z, w2[:, 1]], axis=2)
    return jnp.concatenate([top, bot], axis=1)


def kernel(x, norm1_g, w_in, q_norm_g, k_norm_g, lru_conv_w, lru_conv_b, wa_f, ba_f, wx_f, bx_f, lam_f,
           wa_b, ba_b, wx_b, bx_b, lam_b, w_out, norm2_g, w_up, up_conv_w, up_conv_b, w_down, final_g):
    B, S, D = x.shape
    depth = w_in.shape[0]
    assert depth == 1, "the final RMSNorm is fused into the (single) layer's channel mixer"
    lru_w = lam_f.shape[1]
    n_grp = lru_w // LANES
    cos, sa, sb = _rope_tables(S)
    idx = jnp.arange(MXU_DIM) // HEAD_DIM
    e = (idx[:, None] == idx[None, :]).astype(BF16)

    for l in range(depth):
        gqk = jnp.concatenate([jnp.tile(q_norm_g[l] * (HEAD_DIM ** -0.5 * math.log2(math.e)), N_HEADS),
                               jnp.tile(k_norm_g[l], N_KV_HEADS)])[None, :]
        q, k, vt, xr, gy = _in_proj(x, norm1_g[l][None, :], w_in[l].astype(BF16), gqk, cos, sa, sb, e)
        c = (ATTN_BOUND_MARGIN * HEAD_DIM) * jnp.max(jnp.abs(gqk[:, :ATTN_WIDTH])) \
            * jnp.max(jnp.abs(gqk[:, ATTN_WIDTH:]))
        ot = _attention(q, k, vt, c.reshape(1))

        wg = (0.5 * jnp.concatenate([_block_diag_pairs(w[l]) for w in (wa_f, wx_f, wa_b, wx_b)],
                                    axis=2)).astype(BF16)
        bg = 0.5 * jnp.concatenate([b[l].reshape(n_grp, 1, LANES) for b in (ba_f, bx_f, ba_b, bx_b)],
                                   axis=2)
        lru = _lru(xr, gy, lru_conv_w[l], lru_conv_b[l][None, :], wg, bg,
                   lam_f[l][None, :], lam_b[l][None, :])

        assert lru_w == ATTN_WIDTH, "w_out is fetched as two equal row blocks"
        x1 = _out_proj(ot, lru, x, w_out[l].astype(BF16))
        x = _ffn(x1, norm2_g[l][None, :], w_up[l].astype(BF16), up_conv_w[l], up_conv_b[l][None, :],
                 w_down[l].astype(BF16), final_g[None, :])
    return x
```

```python
import math

import jax
import jax.numpy as jnp
from jax import lax
from jax.experimental import pallas as pl
from jax.experimental.pallas import tpu as pltpu

F32 = jnp.float32
BF16 = jnp.bfloat16

N_HEADS = 8
N_KV_HEADS = 2
HEAD_DIM = 64
GQA_GROUP = N_HEADS // N_KV_HEADS
ATTN_WIDTH = N_HEADS * HEAD_DIM
KV_WIDTH = N_KV_HEADS * HEAD_DIM
LRU_C = 8.0
GRID_W = 64
ROPE_THETA = 10000.0
EPS = 1e-6

LANES = 128
SUBLANES = 8
BF16_ROWS = 16
MXU_DIM = 256
VMEM_LIMIT_BYTES = 56 * 1024 * 1024

PROJ_TM = 1024
OUT_TM = 1024
ATTN_TQ = 256
ATTN_KC = 256
ATTN_MAX_BOUND = 40.0
ATTN_BOUND_MARGIN = 1.03
DEN_ROWS = BF16_ROWS
LRU_TC = 256
FFN_TM = 512
FFN_SUB = 1024


def _gelu_tanh(x):
    return 0.5 * x * (1.0 + jnp.tanh(math.sqrt(2.0 / math.pi) * (x + 0.044715 * (x * x * x))))


def _in_proj_kernel(x_ref, g1_ref, w_ref, gqk_ref, cos_ref, sa_ref, sb_ref, e_ref,
                    q_ref, k_ref, vt_ref, xr_ref, gy_ref):
    x = x_ref[0]
    ms = jnp.mean(x * x, axis=-1, keepdims=True)
    h = (x * lax.rsqrt(ms + EPS) * g1_ref[...]).astype(BF16)

    n_qkv = ATTN_WIDTH + 2 * KV_WIDTH
    z = jnp.dot(h, w_ref[:, 0:n_qkv], preferred_element_type=F32)

    sq = (z * z).astype(BF16)
    e = e_ref[...]
    sums = [jnp.dot(sq[:, c * MXU_DIM:(c + 1) * MXU_DIM], e, preferred_element_type=F32)
            for c in range(n_qkv // MXU_DIM)]
    ssum = jnp.concatenate(sums, axis=1)
    n_qk = ATTN_WIDTH + KV_WIDTH
    qk = z[:, 0:n_qk] * lax.rsqrt(ssum[:, 0:n_qk] * (1.0 / HEAD_DIM) + EPS) * gqk_ref[...]

    cos = cos_ref[...]
    sa = sa_ref[...]
    sb = sb_ref[...]
    roped = []
    for c in range(n_qk // LANES):
        xc = qk[:, c * LANES:(c + 1) * LANES]
        up = pltpu.roll(xc, LANES - 16, 1)
        dn = pltpu.roll(xc, 16, 1)
        roped.append(xc * cos + up * sa + dn * sb)
    for c in range(ATTN_WIDTH // LANES):
        q_ref[0, c * LANES:(c + 1) * LANES, :] = roped[c].T.astype(BF16)
    k_ref[0] = roped[ATTN_WIDTH // LANES].astype(BF16)

    vt_ref[0] = z[:, n_qk:n_qkv].T.astype(BF16)

    lru_w = xr_ref.shape[2]
    xr_ref[0] = jnp.dot(h, w_ref[:, n_qkv:n_qkv + lru_w], preferred_element_type=F32)
    yr = jnp.dot(h, w_ref[:, n_qkv + lru_w:n_qkv + 2 * lru_w], preferred_element_type=F32)
    gy_ref[0] = _gelu_tanh(yr).astype(BF16)


def _in_proj(x, g1, w_in, gqk, cos, sa, sb, e):
    B, S, D = x.shape
    tm = PROJ_TM
    n_in = w_in.shape[1]
    lru_w = (n_in - ATTN_WIDTH - 2 * KV_WIDTH) // 2
    n_qk = ATTN_WIDTH + KV_WIDTH
    grid = (B, S // tm)
    tab = pl.BlockSpec((tm, LANES), lambda b, i: (i, 0))
    return pl.pallas_call(
        _in_proj_kernel,
        grid=grid,
        in_specs=[
            pl.BlockSpec((1, tm, D), lambda b, i: (b, i, 0)),
            pl.BlockSpec((1, D), lambda b, i: (0, 0)),
            pl.BlockSpec((D, n_in), lambda b, i: (0, 0)),
            pl.BlockSpec((1, n_qk), lambda b, i: (0, 0)),
            tab, tab, tab,
            pl.BlockSpec((MXU_DIM, MXU_DIM), lambda b, i: (0, 0)),
        ],
        out_specs=[
            pl.BlockSpec((1, ATTN_WIDTH, tm), lambda b, i: (b, 0, i)),
            pl.BlockSpec((1, tm, KV_WIDTH), lambda b, i: (b, i, 0)),
            pl.BlockSpec((1, KV_WIDTH, tm), lambda b, i: (b, 0, i)),
            pl.BlockSpec((1, tm, lru_w), lambda b, i: (b, i, 0)),
            pl.BlockSpec((1, tm, lru_w), lambda b, i: (b, i, 0)),
        ],
        out_shape=[
            jax.ShapeDtypeStruct((B, ATTN_WIDTH, S), BF16),
            jax.ShapeDtypeStruct((B, S, KV_WIDTH), BF16),
            jax.ShapeDtypeStruct((B, KV_WIDTH, S), BF16),
            jax.ShapeDtypeStruct((B, S, lru_w), F32),
            jax.ShapeDtypeStruct((B, S, lru_w), BF16),
        ],
        compiler_params=pltpu.CompilerParams(
            dimension_semantics=("arbitrary", "arbitrary"),
            vmem_limit_bytes=VMEM_LIMIT_BYTES),
        name="in_proj",
    )(x, g1, w_in, gqk, cos, sa, sb, e)


def _attn_kernel(qt_ref, k_ref, vt_ref, o_ref):
    tq = qt_ref.shape[2]
    S = k_ref.shape[1]
    k = k_ref[0]
    ones = jnp.ones((DEN_ROWS, S), BF16)
    zeros = jnp.zeros((HEAD_DIM, tq), BF16)
    for h in range(N_HEADS):
        j = h // GQA_GROUP
        qt = qt_ref[0, h * HEAD_DIM:(h + 1) * HEAD_DIM, :]
        qpad = jnp.concatenate([qt, zeros] if j == 0 else [zeros, qt], axis=0)
        s = jnp.dot(k, qpad, preferred_element_type=F32)
        m = jnp.max(s, axis=0, keepdims=True)
        p = jnp.exp2(s - m).astype(BF16)
        vext = jnp.concatenate([vt_ref[0, j * HEAD_DIM:(j + 1) * HEAD_DIM, :], ones], axis=0)
        pv = jnp.dot(vext, p, preferred_element_type=F32)
        o = pv[0:HEAD_DIM] / pv[HEAD_DIM:HEAD_DIM + 1]
        o_ref[0, h * HEAD_DIM:(h + 1) * HEAD_DIM, :] = o.astype(BF16)


def _attn_bounded_kernel(c_ref, qt_ref, k_ref, vt_ref, o_ref):
    tq = qt_ref.shape[2]
    S = k_ref.shape[1]
    assert tq == MXU_DIM and ATTN_KC == MXU_DIM
    c = c_ref[0]
    n_chunks = S // ATTN_KC
    n_mxu = 2
    n_rounds = N_HEADS // n_mxu
    n_items = n_rounds * n_chunks
    tile_entries = MXU_DIM // 4
    score_acc = (0, tile_entries)
    out_acc = (2 * tile_entries, 3 * tile_entries)
    quarter = ATTN_KC // 4
    zeros_q = jnp.zeros((HEAD_DIM, tq), BF16)
    zeros_k = jnp.zeros((quarter, MXU_DIM - KV_WIDTH), BF16)

    def q_rhs(r, m):
        h = r * n_mxu + m
        rows = [zeros_q] * (MXU_DIM // HEAD_DIM)
        rows[h // GQA_GROUP] = qt_ref[0, h * HEAD_DIM:(h + 1) * HEAD_DIM, :]
        return jnp.concatenate(rows, axis=0)

    for m in range(n_mxu):
        pltpu.matmul_push_rhs(q_rhs(0, m), staging_register=0, mxu_index=m)
    p_tiles = [None] * n_mxu
    den = [jnp.zeros((SUBLANES, tq), F32) for _ in range(n_mxu)]
    for t in range(n_items + 2):
        if t >= 2:
            for m in range(n_mxu):
                pltpu.matmul_push_rhs(p_tiles[m], staging_register=1, mxu_index=m)
        new_tiles = [[] for _ in range(n_mxu)]
        for qd in range(4):
            if t < n_items:
                ci = t % n_chunks
                rows = slice(ci * ATTN_KC + qd * quarter, ci * ATTN_KC + (qd + 1) * quarter)
                lhs = jnp.concatenate([k_ref[0, rows, :], zeros_k], axis=1)
                for m in range(n_mxu):
                    pltpu.matmul_acc_lhs(score_acc[t % 2] + qd * (quarter // 4), lhs, mxu_index=m,
                                         load_staged_rhs=0 if qd == 0 else None)
            if 1 <= t <= n_items:
                for m in range(n_mxu):
                    s = pltpu.matmul_pop(score_acc[(t - 1) % 2] + qd * (quarter // 4), (quarter, tq), F32,
                                         mxu_index=m)
                    p = jnp.exp2(s - c)
                    den[m] = den[m] + p.reshape(quarter // SUBLANES, SUBLANES, tq).sum(axis=0)
                    new_tiles[m].append(p.astype(BF16))
        if t + 1 < n_items:
            for m in range(n_mxu):
                pltpu.matmul_push_rhs(q_rhs((t + 1) // n_chunks, m), staging_register=0, mxu_index=m)
        if t >= 2:
            r, cj = divmod(t - 2, n_chunks)
            for m in range(n_mxu):
                h = r * n_mxu + m
                j = h // GQA_GROUP
                vchunk = vt_ref[0, j * HEAD_DIM:(j + 1) * HEAD_DIM, cj * ATTN_KC:(cj + 1) * ATTN_KC]
                pltpu.matmul_acc_lhs(out_acc[r % 2], vchunk, mxu_index=m, load_staged_rhs=1)
                if cj == n_chunks - 1:
                    acc = pltpu.matmul_pop(out_acc[r % 2], (HEAD_DIM, tq), F32, mxu_index=m)
                    o = acc / den_done[m]
                    o_ref[0, h * HEAD_DIM:(h + 1) * HEAD_DIM, :] = o.astype(BF16)
        if 1 <= t <= n_items:
            p_tiles = [jnp.concatenate(tl, axis=0) for tl in new_tiles]
            if (t - 1) % n_chunks == n_chunks - 1:
                den_done = [jnp.sum(d, axis=0, keepdims=True) for d in den]
                den = [jnp.zeros((SUBLANES, tq), F32) for _ in range(n_mxu)]


def _attention(q, k, vt, c):
    B, _, S = q.shape
    tq = ATTN_TQ
    specs = [
        pl.BlockSpec((1, ATTN_WIDTH, tq), lambda b, i: (b, 0, i)),
        pl.BlockSpec((1, S, KV_WIDTH), lambda b, i: (b, 0, 0)),
        pl.BlockSpec((1, KV_WIDTH, S), lambda b, i: (b, 0, 0)),
    ]
    common = dict(
        grid=(B, S // tq),
        out_specs=pl.BlockSpec((1, ATTN_WIDTH, tq), lambda b, i: (b, 0, i)),
        out_shape=jax.ShapeDtypeStruct((B, ATTN_WIDTH, S), BF16),
        compiler_params=pltpu.CompilerParams(
            dimension_semantics=("arbitrary", "arbitrary"),
            vmem_limit_bytes=VMEM_LIMIT_BYTES),
    )
    bounded = pl.pallas_call(
        _attn_bounded_kernel,
        in_specs=[pl.BlockSpec(memory_space=pltpu.SMEM)] + specs,
        name="attention_bounded", **common)
    exact = pl.pallas_call(_attn_kernel, in_specs=specs, name="attention", **common)
    return lax.cond(c[0] <= ATTN_MAX_BOUND,
                    lambda: bounded(c, q, k, vt),
                    lambda: exact(q, k, vt))


def _lru_kernel(x_ref, gy_ref, cw_ref, cb_ref, wg_ref, bg_ref, lamf_ref, lamb_ref,
                o_ref, xpad, af, bf, ab, bb, hf_s, pf_s, hb_s, pb_s, cf_ref, cr_ref):
    S = x_ref.shape[1]
    W = x_ref.shape[2]
    tc = LRU_TC
    pad = SUBLANES
    seg = S // SUBLANES
    pitch = seg + SUBLANES

    xpad[0:pad, :] = jnp.zeros((pad, W), F32)
    xpad[pad + S:pad + S + pad, :] = jnp.zeros((pad, W), F32)
    xpad[pad:pad + S, :] = x_ref[0]

    def log_sigmoid(v):
        return -(jnp.maximum(-v, 0.0) + jnp.log1p(jnp.exp(-jnp.abs(v))))

    hl_f = (0.5 * LRU_C) * log_sigmoid(lamf_ref[...])
    hl_b = (0.5 * LRU_C) * log_sigmoid(lamb_ref[...])
    cw = cw_ref[...]
    cb = cb_ref[...]
    bg = bg_ref[0]
    per_seg = seg // tc
    n_chunks = S // tc

    def chunk_row(ci):
        return pl.multiple_of((ci // per_seg) * pitch + (ci % per_seg) * tc, SUBLANES)

    def gates(ci, first_row, last_row):
        t0 = pl.multiple_of(ci * tc, tc)
        xw = xpad[pl.ds(t0, tc + 2 * pad), :]
        xc = (cw[0:1] * xw[pad - 2:pad - 2 + tc] + cw[1:2] * xw[pad - 1:pad - 1 + tc]
              + cw[2:3] * xw[pad:pad + tc] + cw[3:4] * xw[pad + 1:pad + 1 + tc] + cb)
        g = jnp.dot(xc.astype(BF16), wg_ref[0], preferred_element_type=F32) + bg
        xh = 0.5 * xc
        r0 = chunk_row(ci)
        trow = lax.broadcasted_iota(jnp.int32, (tc, 1), 0)
        for d, (hl, a_s, b_s, unit) in enumerate(((hl_f, af, bf, 0 if first_row else None),
                                                  (hl_b, ab, bb, tc - 1 if last_row else None))):
            tr = jnp.tanh(g[:, (2 * d) * W:(2 * d + 1) * W])
            ti = jnp.tanh(g[:, (2 * d + 1) * W:(2 * d + 2) * W])
            log_a = tr * hl + hl
            a = jnp.exp(log_a)
            m2 = jnp.tanh(log_a) * (-1.0 - a * a)
            mult = m2 * lax.rsqrt(jnp.maximum(m2, 1e-30))
            if unit is not None:
                mult = jnp.where(trow == unit, 1.0, mult)
            a_s[pl.ds(r0, tc), :] = a
            b_s[pl.ds(r0, tc), :] = mult * (ti * xh + xh)
        return 0

    gates(0, True, False)
    lax.fori_loop(1, n_chunks - 1, lambda ci, _: gates(ci, False, False), 0, unroll=2)
    gates(n_chunks - 1, False, True)

    def step(a_s, b_s, h_s, p_s, r, h, A):
        rows = pl.ds(r, SUBLANES, stride=pitch)
        a = a_s[rows, :]
        h = a * h + b_s[rows, :]
        A = a * A
        h_s[rows, :] = h
        p_s[rows, :] = A
        return h, A

    def scan(r, carry):
        hf, Af, hb, Ab = carry
        hf, Af = step(af, bf, hf_s, pf_s, r, hf, Af)
        hb, Ab = step(ab, bb, hb_s, pb_s, seg - 1 - r, hb, Ab)
        return hf, Af, hb, Ab

    z8 = jnp.zeros((SUBLANES, W), F32)
    o8 = jnp.ones((SUBLANES, W), F32)
    hf, Af, hb, Ab = lax.fori_loop(0, seg, scan, (z8, o8, z8, o8), unroll=8)

    c = jnp.zeros((1, W), F32)
    rows = [c]
    for sgm in range(SUBLANES - 1):
        c = hf[sgm:sgm + 1] + Af[sgm:sgm + 1] * c
        rows.append(c)
    cf_ref[...] = jnp.concatenate(rows, axis=0)
    c = jnp.zeros((1, W), F32)
    rows = [c]
    for sgm in range(SUBLANES - 1, 0, -1):
        c = hb[sgm:sgm + 1] + Ab[sgm:sgm + 1] * c
        rows.append(c)
    cr_ref[...] = jnp.concatenate(rows[::-1], axis=0)

    def combine(ci, _):
        t0 = pl.multiple_of(ci * tc, tc)
        r0 = chunk_row(ci)
        sgm = ci // per_seg
        h_f = hf_s[pl.ds(r0, tc), :] + pf_s[pl.ds(r0, tc), :] * cf_ref[pl.ds(sgm, 1), :]
        h_b = hb_s[pl.ds(r0, tc), :] + pb_s[pl.ds(r0, tc), :] * cr_ref[pl.ds(sgm, 1), :]
        o_ref[0, pl.ds(t0, tc), :] = ((h_f + h_b) * gy_ref[0, pl.ds(t0, tc), :].astype(F32)).astype(BF16)
        return 0

    lax.fori_loop(0, n_chunks, combine, 0)


def _lru(xr, gy, conv_w, conv_b, wg, bg, lam_f, lam_b):
    B, S, C = xr.shape
    W = LANES
    n_grp = C // W
    seq = pl.BlockSpec((1, S, W), lambda b, c: (b, 0, c))
    vec = pl.BlockSpec((1, W), lambda b, c: (0, c))
    seg_rows = S + SUBLANES * SUBLANES
    return pl.pallas_call(
        _lru_kernel,
        grid=(B, n_grp),
        in_specs=[
            seq, seq,
            pl.BlockSpec((conv_w.shape[0], W), lambda b, c: (0, c)),
            vec,
            pl.BlockSpec((1, W, 4 * W), lambda b, c: (c, 0, 0)),
            pl.BlockSpec((1, 1, 4 * W), lambda b, c: (c, 0, 0)),
            vec, vec,
        ],
        out_specs=seq,
        out_shape=jax.ShapeDtypeStruct((B, S, C), BF16),
        scratch_shapes=[
            pltpu.VMEM((S + 2 * SUBLANES, W), F32),
            pltpu.VMEM((seg_rows, W), F32), pltpu.VMEM((seg_rows, W), F32),
            pltpu.VMEM((seg_rows, W), F32), pltpu.VMEM((seg_rows, W), F32),
            pltpu.VMEM((seg_rows, W), F32), pltpu.VMEM((seg_rows, W), F32),
            pltpu.VMEM((seg_rows, W), F32), pltpu.VMEM((seg_rows, W), F32),
            pltpu.VMEM((SUBLANES, W), F32), pltpu.VMEM((SUBLANES, W), F32),
        ],
        compiler_params=pltpu.CompilerParams(
            dimension_semantics=("arbitrary", "arbitrary"),
            vmem_limit_bytes=VMEM_LIMIT_BYTES),
        name="rg_lru",
    )(xr, gy, conv_w, conv_b, wg, bg, lam_f, lam_b)


def _out_proj_kernel(ot_ref, lru_ref, x_ref, wa_ref, wl_ref, x1_ref):
    mixed = lax.dot_general(ot_ref[0], wa_ref[...], (((0,), (0,)), ((), ())),
                            preferred_element_type=F32)
    mixed = mixed + jnp.dot(lru_ref[0], wl_ref[...], preferred_element_type=F32)
    x1_ref[0] = x_ref[0] + mixed


def _out_proj(ot, lru, x, wo):
    B, S, D = x.shape
    tm = OUT_TM
    return pl.pallas_call(
        _out_proj_kernel,
        grid=(B, S // tm),
        in_specs=[
            pl.BlockSpec((1, ATTN_WIDTH, tm), lambda b, i: (b, 0, i)),
            pl.BlockSpec((1, tm, lru.shape[2]), lambda b, i: (b, i, 0)),
            pl.BlockSpec((1, tm, D), lambda b, i: (b, i, 0)),
            pl.BlockSpec((ATTN_WIDTH, D), lambda b, i: (0, 0)),
            pl.BlockSpec((lru.shape[2], D), lambda b, i: (1, 0)),
        ],
        out_specs=pl.BlockSpec((1, tm, D), lambda b, i: (b, i, 0)),
        out_shape=jax.ShapeDtypeStruct((B, S, D), F32),
        compiler_params=pltpu.CompilerParams(
            dimension_semantics=("arbitrary", "arbitrary"),
            vmem_limit_bytes=VMEM_LIMIT_BYTES),
        name="out_proj",
    )(ot, lru, x, wo, wo)


def _ffn_kernel(xp_ref, x1_ref, xn_ref, g2_ref, wg_ref, wv_ref, cwg_ref, cwv_ref, cbg_ref, cbv_ref,
                wd_ref, gf_ref, o_ref, xs_ref, acc_ref):
    i = pl.program_id(1)
    tm = x1_ref.shape[1]
    gr = tm // SUBLANES
    n_slab = x1_ref.shape[2] // LANES

    def group(s):
        return jnp.concatenate([xs_ref[k, pl.ds(s, gr, stride=SUBLANES), :] for k in range(n_slab)], axis=1)

    def norm(xs, g):
        ms = jnp.mean(xs * xs, axis=-1, keepdims=True)
        return xs * lax.rsqrt(ms + EPS) * g

    g2 = g2_ref[...]
    for k in range(n_slab):
        xs_ref[k] = x1_ref[0, :, k * LANES:(k + 1) * LANES]
    keep_p = jnp.where(i == 0, 0.0, 1.0)
    keep_n = jnp.where(i == pl.num_programs(1) - 1, 0.0, 1.0)
    halo = jnp.concatenate([norm(xp_ref[0], g2) * keep_p, norm(xn_ref[0], g2) * keep_n], axis=0)
    h = jnp.concatenate([norm(group(s), g2).astype(BF16) for s in range(SUBLANES)]
                        + [halo.astype(BF16)], axis=0)
    n_sub = wg_ref.shape[1] // FFN_SUB

    def up(c):
        cols = slice(c * FFN_SUB, (c + 1) * FFN_SUB)
        return (jnp.dot(h, wg_ref[:, cols], preferred_element_type=F32),
                jnp.dot(h, wv_ref[:, cols], preferred_element_type=F32))

    def conv(u, cw, cb):
        grp = [u[s * gr:(s + 1) * gr] for s in range(SUBLANES)]
        t_prev = u[tm + SUBLANES - 1:tm + SUBLANES]
        t_next = u[tm + SUBLANES:tm + SUBLANES + 1]
        before = jnp.concatenate([t_prev, grp[SUBLANES - 1][:gr - 1]], axis=0)
        after = jnp.concatenate([grp[0][1:], t_next], axis=0)
        outs = []
        for s in range(SUBLANES):
            prv = before if s == 0 else grp[s - 1]
            nxt = after if s == SUBLANES - 1 else grp[s + 1]
            outs.append(cw[0:1] * prv + cw[1:2] * grp[s] + cw[2:3] * nxt + cb)
        return outs

    u_next = up(0)
    for c in range(n_sub):
        ug, uv = u_next
        if c + 1 < n_sub:
            u_next = up(c + 1)
        cols = slice(c * FFN_SUB, (c + 1) * FFN_SUB)
        gate = conv(ug, cwg_ref[:, cols], cbg_ref[:, cols])
        val = conv(uv, cwv_ref[:, cols], cbv_ref[:, cols])
        act = jnp.concatenate([(_gelu_tanh(g) * v).astype(BF16) for g, v in zip(gate, val)], axis=0)
        d = jnp.dot(act, wd_ref[c * FFN_SUB:(c + 1) * FFN_SUB, :], preferred_element_type=F32)
        if c == 0:
            acc_ref[...] = d
        else:
            acc_ref[...] += d

    gf = gf_ref[...]
    for s in range(SUBLANES):
        y = norm(group(s) + acc_ref[s * gr:(s + 1) * gr, :], gf)
        for k in range(n_slab):
            xs_ref[k, pl.ds(s, gr, stride=SUBLANES), :] = y[:, k * LANES:(k + 1) * LANES]
    for k in range(n_slab):
        o_ref[0, :, k * LANES:(k + 1) * LANES] = xs_ref[k]


def _ffn(x1, g2, w_up, cw, cb, w_down, gf):
    B, S, D = x1.shape
    tm = FFN_TM
    d_ff = w_down.shape[0]
    hb = tm // SUBLANES
    n_hb = S // SUBLANES
    resident = dict(pipeline_mode=pl.Buffered(1))
    return pl.pallas_call(
        _ffn_kernel,
        grid=(B, S // tm),
        in_specs=[
            pl.BlockSpec((1, SUBLANES, D), lambda b, i: (b, jnp.maximum(i * hb - 1, 0), 0)),
            pl.BlockSpec((1, tm, D), lambda b, i: (b, i, 0)),
            pl.BlockSpec((1, SUBLANES, D), lambda b, i: (b, jnp.minimum((i + 1) * hb, n_hb - 1), 0)),
            pl.BlockSpec((1, D), lambda b, i: (0, 0)),
            pl.BlockSpec((D, d_ff), lambda b, i: (0, 0), **resident),
            pl.BlockSpec((D, d_ff), lambda b, i: (0, 1), **resident),
            pl.BlockSpec((cw.shape[0], d_ff), lambda b, i: (0, 0)),
            pl.BlockSpec((cw.shape[0], d_ff), lambda b, i: (0, 1)),
            pl.BlockSpec((1, d_ff), lambda b, i: (0, 0)),
            pl.BlockSpec((1, d_ff), lambda b, i: (0, 1)),
            pl.BlockSpec((d_ff, D), lambda b, i: (0, 0), **resident),
            pl.BlockSpec((1, D), lambda b, i: (0, 0)),
        ],
        out_specs=pl.BlockSpec((1, tm, D), lambda b, i: (b, i, 0)),
        out_shape=jax.ShapeDtypeStruct((B, S, D), F32),
        scratch_shapes=[pltpu.VMEM((D // LANES, tm, LANES), F32), pltpu.VMEM((tm, D), F32)],
        compiler_params=pltpu.CompilerParams(
            dimension_semantics=("arbitrary", "arbitrary"),
            vmem_limit_bytes=VMEM_LIMIT_BYTES),
        name="conv_ffn",
    )(x1, x1, x1, g2, w_up, w_up, cw, cw, cb, cb, w_down, gf)


def _rope_tables(S):
    t = jnp.arange(S, dtype=jnp.int32)
    pos = jnp.stack([t // GRID_W, t % GRID_W], axis=1).astype(F32)
    quarter = HEAD_DIM // 4
    inv_freq = 1.0 / (ROPE_THETA ** (jnp.arange(0, 2 * quarter, 2, dtype=F32) / (2 * quarter)))
    ang = pos[:, :, None] * inv_freq[None, None, :]
    cos = jnp.cos(ang)
    sin = jnp.sin(ang)
    zero = jnp.zeros_like(sin)
    cos_h = jnp.concatenate([cos, cos], axis=2).reshape(S, HEAD_DIM)
    sa_h = jnp.concatenate([-sin, zero], axis=2).reshape(S, HEAD_DIM)
    sb_h = jnp.concatenate([zero, sin], axis=2).reshape(S, HEAD_DIM)
    rep = LANES // HEAD_DIM
    return (jnp.tile(cos_h, (1, rep)), jnp.tile(sa_h, (1, rep)), jnp.tile(sb_h, (1, rep)))


def _block_diag_pairs(w):
    nb, bs, _ = w.shape
    w2 = w.reshape(nb // 2, 2, bs, bs)
    z = jnp.zeros((nb // 2, bs, bs), w.dtype)
    top = jnp.concatenate([w2[:, 0], z], axis=2)
    bot = jnp.concatenate([z, w2[:, 1]], axis=2)
    return jnp.concatenate([top, bot], axis=1)


def kernel(x, norm1_g, w_in, q_norm_g, k_norm_g, lru_conv_w, lru_conv_b, wa_f, ba_f, wx_f, bx_f, lam_f,
           wa_b, ba_b, wx_b, bx_b, lam_b, w_out, norm2_g, w_up, up_conv_w, up_conv_b, w_down, final_g):
    B, S, D = x.shape
    depth = w_in.shape[0]
    assert depth == 1, "the final RMSNorm is fused into the (single) layer's channel mixer"
    lru_w = lam_f.shape[1]
    n_grp = lru_w // LANES
    cos, sa, sb = _rope_tables(S)
    idx = jnp.arange(MXU_DIM) // HEAD_DIM
    e = (idx[:, None] == idx[None, :]).astype(BF16)

    for l in range(depth):
        gqk = jnp.concatenate([jnp.tile(q_norm_g[l] * (HEAD_DIM ** -0.5 * math.log2(math.e)), N_HEADS),
                               jnp.tile(k_norm_g[l], N_KV_HEADS)])[None, :]
        q, k, vt, xr, gy = _in_proj(x, norm1_g[l][None, :], w_in[l].astype(BF16), gqk, cos, sa, sb, e)
        c = (ATTN_BOUND_MARGIN * HEAD_DIM) * jnp.max(jnp.abs(gqk[:, :ATTN_WIDTH])) \
            * jnp.max(jnp.abs(gqk[:, ATTN_WIDTH:]))
        ot = _attention(q, k, vt, c.reshape(1))

        wg = (0.5 * jnp.concatenate([_block_diag_pairs(w[l]) for w in (wa_f, wx_f, wa_b, wx_b)],
                                    axis=2)).astype(BF16)
        bg = 0.5 * jnp.concatenate([b[l].reshape(n_grp, 1, LANES) for b in (ba_f, bx_f, ba_b, bx_b)],
                                   axis=2)
        lru = _lru(xr, gy, lru_conv_w[l], lru_conv_b[l][None, :], wg, bg,
                   lam_f[l][None, :], lam_b[l][None, :])

        assert lru_w == ATTN_WIDTH, "w_out is fetched as two equal row blocks"
        x1 = _out_proj(ot, lru, x, w_out[l].astype(BF16))
        x = _ffn(x1, norm2_g[l][None, :], w_up[l].astype(BF16), up_conv_w[l], up_conv_b[l][None, :],
                 w_down[l].astype(BF16), final_g[None, :])
    return x
```

```python
import math

import jax
import jax.numpy as jnp
from jax import lax
from jax.experimental import pallas as pl
from jax.experimental.pallas import tpu as pltpu

F32 = jnp.float32
BF16 = jnp.bfloat16

N_HEADS = 8
N_KV_HEADS = 2
HEAD_DIM = 64
GQA_GROUP = N_HEADS // N_KV_HEADS
ATTN_WIDTH = N_HEADS * HEAD_DIM
KV_WIDTH = N_KV_HEADS * HEAD_DIM
LRU_C = 8.0
GRID_W = 64
ROPE_THETA = 10000.0
EPS = 1e-6

LANES = 128
SUBLANES = 8
BF16_ROWS = 16
MXU_DIM = 256
VMEM_LIMIT_BYTES = 56 * 1024 * 1024

PROJ_TM = 1024
OUT_TM = 1024
ATTN_TQ = 256
ATTN_KC = 256
ATTN_MAX_BOUND = 40.0
ATTN_BOUND_MARGIN = 1.03
DEN_ROWS = BF16_ROWS
LRU_TC = 256
FFN_TM = 1024
FFN_SUB = 1024


def _gelu_tanh(x):
    return 0.5 * x * (1.0 + jnp.tanh(math.sqrt(2.0 / math.pi) * (x + 0.044715 * (x * x * x))))


def _in_proj_kernel(x_ref, g1_ref, w_ref, gqk_ref, cos_ref, sa_ref, sb_ref, e_ref,
                    q_ref, k_ref, vt_ref, xr_ref, gy_ref):
    x = x_ref[0]
    ms = jnp.mean(x * x, axis=-1, keepdims=True)
    h = (x * lax.rsqrt(ms + EPS) * g1_ref[...]).astype(BF16)

    n_qkv = ATTN_WIDTH + 2 * KV_WIDTH
    z = jnp.dot(h, w_ref[:, 0:n_qkv], preferred_element_type=F32)

    sq = (z * z).astype(BF16)
    e = e_ref[...]
    sums = [jnp.dot(sq[:, c * MXU_DIM:(c + 1) * MXU_DIM], e, preferred_element_type=F32)
            for c in range(n_qkv // MXU_DIM)]
    ssum = jnp.concatenate(sums, axis=1)
    n_qk = ATTN_WIDTH + KV_WIDTH
    qk = z[:, 0:n_qk] * lax.rsqrt(ssum[:, 0:n_qk] * (1.0 / HEAD_DIM) + EPS) * gqk_ref[...]

    cos = cos_ref[...]
    sa = sa_ref[...]
    sb = sb_ref[...]
    roped = []
    for c in range(n_qk // LANES):
        xc = qk[:, c * LANES:(c + 1) * LANES]
        up = pltpu.roll(xc, LANES - 16, 1)
        dn = pltpu.roll(xc, 16, 1)
        roped.append(xc * cos + up * sa + dn * sb)
    for c in range(ATTN_WIDTH // LANES):
        q_ref[0, c * LANES:(c + 1) * LANES, :] = roped[c].T.astype(BF16)
    k_ref[0] = roped[ATTN_WIDTH // LANES].astype(BF16)

    vt_ref[0] = z[:, n_qk:n_qkv].T.astype(BF16)

    lru_w = xr_ref.shape[2]
    xr_ref[0] = jnp.dot(h, w_ref[:, n_qkv:n_qkv + lru_w], preferred_element_type=F32)
    yr = jnp.dot(h, w_ref[:, n_qkv + lru_w:n_qkv + 2 * lru_w], preferred_element_type=F32)
    gy_ref[0] = _gelu_tanh(yr).astype(BF16)


def _in_proj(x, g1, w_in, gqk, cos, sa, sb, e):
    B, S, D = x.shape
    tm = PROJ_TM
    n_in = w_in.shape[1]
    lru_w = (n_in - ATTN_WIDTH - 2 * KV_WIDTH) // 2
    n_qk = ATTN_WIDTH + KV_WIDTH
    grid = (B, S // tm)
    tab = pl.BlockSpec((tm, LANES), lambda b, i: (i, 0))
    return pl.pallas_call(
        _in_proj_kernel,
        grid=grid,
        in_specs=[
            pl.BlockSpec((1, tm, D), lambda b, i: (b, i, 0)),
            pl.BlockSpec((1, D), lambda b, i: (0, 0)),
            pl.BlockSpec((D, n_in), lambda b, i: (0, 0)),
            pl.BlockSpec((1, n_qk), lambda b, i: (0, 0)),
            tab, tab, tab,
            pl.BlockSpec((MXU_DIM, MXU_DIM), lambda b, i: (0, 0)),
        ],
        out_specs=[
            pl.BlockSpec((1, ATTN_WIDTH, tm), lambda b, i: (b, 0, i)),
            pl.BlockSpec((1, tm, KV_WIDTH), lambda b, i: (b, i, 0)),
            pl.BlockSpec((1, KV_WIDTH, tm), lambda b, i: (b, 0, i)),
            pl.BlockSpec((1, tm, lru_w), lambda b, i: (b, i, 0)),
            pl.BlockSpec((1, tm, lru_w), lambda b, i: (b, i, 0)),
        ],
        out_shape=[
            jax.ShapeDtypeStruct((B, ATTN_WIDTH, S), BF16),
            jax.ShapeDtypeStruct((B, S, KV_WIDTH), BF16),
            jax.ShapeDtypeStruct((B, KV_WIDTH, S), BF16),
            jax.ShapeDtypeStruct((B, S, lru_w), F32),
            jax.ShapeDtypeStruct((B, S, lru_w), BF16),
        ],
        compiler_params=pltpu.CompilerParams(
            dimension_semantics=("arbitrary", "arbitrary"),
            vmem_limit_bytes=VMEM_LIMIT_BYTES),
        name="in_proj",
    )(x, g1, w_in, gqk, cos, sa, sb, e)


def _attn_kernel(qt_ref, k_ref, vt_ref, o_ref):
    tq = qt_ref.shape[2]
    S = k_ref.shape[1]
    k = k_ref[0]
    ones = jnp.ones((DEN_ROWS, S), BF16)
    zeros = jnp.zeros((HEAD_DIM, tq), BF16)
    for h in range(N_HEADS):
        j = h // GQA_GROUP
        qt = qt_ref[0, h * HEAD_DIM:(h + 1) * HEAD_DIM, :]
        qpad = jnp.concatenate([qt, zeros] if j == 0 else [zeros, qt], axis=0)
        s = jnp.dot(k, qpad, preferred_element_type=F32)
        m = jnp.max(s, axis=0, keepdims=True)
        p = jnp.exp2(s - m).astype(BF16)
        vext = jnp.concatenate([vt_ref[0, j * HEAD_DIM:(j + 1) * HEAD_DIM, :], ones], axis=0)
        pv = jnp.dot(vext, p, preferred_element_type=F32)
        o = pv[0:HEAD_DIM] / pv[HEAD_DIM:HEAD_DIM + 1]
        o_ref[0, h * HEAD_DIM:(h + 1) * HEAD_DIM, :] = o.astype(BF16)


def _attn_bounded_kernel(c_ref, qt_ref, k_ref, vt_ref, o_ref):
    tq = qt_ref.shape[2]
    S = k_ref.shape[1]
    assert tq == MXU_DIM and ATTN_KC == MXU_DIM
    c = c_ref[0]
    n_chunks = S // ATTN_KC
    n_mxu = 2
    n_rounds = N_HEADS // n_mxu
    n_items = n_rounds * n_chunks
    tile_entries = MXU_DIM // 4
    score_acc = (0, tile_entries)
    out_acc = (2 * tile_entries, 3 * tile_entries)
    quarter = ATTN_KC // 4
    zeros_q = jnp.zeros((HEAD_DIM, tq), BF16)
    zeros_k = jnp.zeros((quarter, MXU_DIM - KV_WIDTH), BF16)

    def q_rhs(r, m):
        h = r * n_mxu + m
        rows = [zeros_q] * (MXU_DIM // HEAD_DIM)
        rows[h // GQA_GROUP] = qt_ref[0, h * HEAD_DIM:(h + 1) * HEAD_DIM, :]
        return jnp.concatenate(rows, axis=0)

    for m in range(n_mxu):
        pltpu.matmul_push_rhs(q_rhs(0, m), staging_register=0, mxu_index=m)
    p_tiles = [None] * n_mxu
    den = [jnp.zeros((SUBLANES, tq), F32) for _ in range(n_mxu)]
    for t in range(n_items + 2):
        if t >= 2:
            for m in range(n_mxu):
                pltpu.matmul_push_rhs(p_tiles[m], staging_register=1, mxu_index=m)
        new_tiles = [[] for _ in range(n_mxu)]
        for qd in range(4):
            if t < n_items:
                ci = t % n_chunks
                rows = slice(ci * ATTN_KC + qd * quarter, ci * ATTN_KC + (qd + 1) * quarter)
                lhs = jnp.concatenate([k_ref[0, rows, :], zeros_k], axis=1)
                for m in range(n_mxu):
                    pltpu.matmul_acc_lhs(score_acc[t % 2] + qd * (quarter // 4), lhs, mxu_index=m,
                                         load_staged_rhs=0 if qd == 0 else None)
            if 1 <= t <= n_items:
                for m in range(n_mxu):
                    s = pltpu.matmul_pop(score_acc[(t - 1) % 2] + qd * (quarter // 4), (quarter, tq), F32,
                                         mxu_index=m)
                    p = jnp.exp2(s - c)
                    den[m] = den[m] + p.reshape(quarter // SUBLANES, SUBLANES, tq).sum(axis=0)
                    new_tiles[m].append(p.astype(BF16))
        if t + 1 < n_items:
            for m in range(n_mxu):
                pltpu.matmul_push_rhs(q_rhs((t + 1) // n_chunks, m), staging_register=0, mxu_index=m)
        if t >= 2:
            r, cj = divmod(t - 2, n_chunks)
            for m in range(n_mxu):
                h = r * n_mxu + m
                j = h // GQA_GROUP
                vchunk = vt_ref[0, j * HEAD_DIM:(j + 1) * HEAD_DIM, cj * ATTN_KC:(cj + 1) * ATTN_KC]
                pltpu.matmul_acc_lhs(out_acc[r % 2], vchunk, mxu_index=m, load_staged_rhs=1)
                if cj == n_chunks - 1:
                    acc = pltpu.matmul_pop(out_acc[r % 2], (HEAD_DIM, tq), F32, mxu_index=m)
                    o = acc / den_done[m]
                    o_ref[0, h * HEAD_DIM:(h + 1) * HEAD_DIM, :] = o.astype(BF16)
        if 1 <= t <= n_items:
            p_tiles = [jnp.concatenate(tl, axis=0) for tl in new_tiles]
            if (t - 1) % n_chunks == n_chunks - 1:
                den_done = [jnp.sum(d, axis=0, keepdims=True) for d in den]
                den = [jnp.zeros((SUBLANES, tq), F32) for _ in range(n_mxu)]


def _attention(q, k, vt, c):
    B, _, S = q.shape
    tq = ATTN_TQ
    specs = [
        pl.BlockSpec((1, ATTN_WIDTH, tq), lambda b, i: (b, 0, i)),
        pl.BlockSpec((1, S, KV_WIDTH), lambda b, i: (b, 0, 0)),
        pl.BlockSpec((1, KV_WIDTH, S), lambda b, i: (b, 0, 0)),
    ]
    common = dict(
        grid=(B, S // tq),
        out_specs=pl.BlockSpec((1, ATTN_WIDTH, tq), lambda b, i: (b, 0, i)),
        out_shape=jax.ShapeDtypeStruct((B, ATTN_WIDTH, S), BF16),
        compiler_params=pltpu.CompilerParams(
            dimension_semantics=("arbitrary", "arbitrary"),
            vmem_limit_bytes=VMEM_LIMIT_BYTES),
    )
    bounded = pl.pallas_call(
        _attn_bounded_kernel,
        in_specs=[pl.BlockSpec(memory_space=pltpu.SMEM)] + specs,
        name="attention_bounded", **common)
    exact = pl.pallas_call(_attn_kernel, in_specs=specs, name="attention", **common)
    return lax.cond(c[0] <= ATTN_MAX_BOUND,
                    lambda: bounded(c, q, k, vt),
                    lambda: exact(q, k, vt))


def _lru_kernel(x_ref, gy_ref, cw_ref, cb_ref, wg_ref, bg_ref, lamf_ref, lamb_ref,
                o_ref, xpad, af, bf, ab, bb, hf_s, pf_s, hb_s, pb_s, cf_ref, cr_ref):
    S = x_ref.shape[1]
    W = x_ref.shape[2]
    tc = LRU_TC
    pad = SUBLANES
    seg = S // SUBLANES
    pitch = seg + SUBLANES

    xpad[0:pad, :] = jnp.zeros((pad, W), F32)
    xpad[pad + S:pad + S + pad, :] = jnp.zeros((pad, W), F32)
    xpad[pad:pad + S, :] = x_ref[0]

    def log_sigmoid(v):
        return -(jnp.maximum(-v, 0.0) + jnp.log1p(jnp.exp(-jnp.abs(v))))

    hl_f = (0.5 * LRU_C) * log_sigmoid(lamf_ref[...])
    hl_b = (0.5 * LRU_C) * log_sigmoid(lamb_ref[...])
    cw = cw_ref[...]
    cb = cb_ref[...]
    bg = bg_ref[0]
    per_seg = seg // tc
    n_chunks = S // tc

    def chunk_row(ci):
        return pl.multiple_of((ci // per_seg) * pitch + (ci % per_seg) * tc, SUBLANES)

    def gates(ci, first_row, last_row):
        t0 = pl.multiple_of(ci * tc, tc)
        xw = xpad[pl.ds(t0, tc + 2 * pad), :]
        xc = (cw[0:1] * xw[pad - 2:pad - 2 + tc] + cw[1:2] * xw[pad - 1:pad - 1 + tc]
              + cw[2:3] * xw[pad:pad + tc] + cw[3:4] * xw[pad + 1:pad + 1 + tc] + cb)
        g = jnp.dot(xc.astype(BF16), wg_ref[0], preferred_element_type=F32) + bg
        xh = 0.5 * xc
        r0 = chunk_row(ci)
        trow = lax.broadcasted_iota(jnp.int32, (tc, 1), 0)
        for d, (hl, a_s, b_s, unit) in enumerate(((hl_f, af, bf, 0 if first_row else None),
                                                  (hl_b, ab, bb, tc - 1 if last_row else None))):
            tr = jnp.tanh(g[:, (2 * d) * W:(2 * d + 1) * W])
            ti = jnp.tanh(g[:, (2 * d + 1) * W:(2 * d + 2) * W])
            log_a = tr * hl + hl
            a = jnp.exp(log_a)
            m2 = jnp.tanh(log_a) * (-1.0 - a * a)
            mult = m2 * lax.rsqrt(jnp.maximum(m2, 1e-30))
            if unit is not None:
                mult = jnp.where(trow == unit, 1.0, mult)
            a_s[pl.ds(r0, tc), :] = a
            b_s[pl.ds(r0, tc), :] = mult * (ti * xh + xh)
        return 0

    gates(0, True, False)
    lax.fori_loop(1, n_chunks - 1, lambda ci, _: gates(ci, False, False), 0, unroll=2)
    gates(n_chunks - 1, False, True)

    def step(a_s, b_s, h_s, p_s, r, h, A):
        rows = pl.ds(r, SUBLANES, stride=pitch)
        a = a_s[rows, :]
        h = a * h + b_s[rows, :]
        A = a * A
        h_s[rows, :] = h
        p_s[rows, :] = A
        return h, A

    def scan(r, carry):
        hf, Af, hb, Ab = carry
        hf, Af = step(af, bf, hf_s, pf_s, r, hf, Af)
        hb, Ab = step(ab, bb, hb_s, pb_s, seg - 1 - r, hb, Ab)
        return hf, Af, hb, Ab

    z8 = jnp.zeros((SUBLANES, W), F32)
    o8 = jnp.ones((SUBLANES, W), F32)
    hf, Af, hb, Ab = lax.fori_loop(0, seg, scan, (z8, o8, z8, o8), unroll=8)

    c = jnp.zeros((1, W), F32)
    rows = [c]
    for sgm in range(SUBLANES - 1):
        c = hf[sgm:sgm + 1] + Af[sgm:sgm + 1] * c
        rows.append(c)
    cf_ref[...] = jnp.concatenate(rows, axis=0)
    c = jnp.zeros((1, W), F32)
    rows = [c]
    for sgm in range(SUBLANES - 1, 0, -1):
        c = hb[sgm:sgm + 1] + Ab[sgm:sgm + 1] * c
        rows.append(c)
    cr_ref[...] = jnp.concatenate(rows[::-1], axis=0)

    def combine(ci, _):
        t0 = pl.multiple_of(ci * tc, tc)
        r0 = chunk_row(ci)
        sgm = ci // per_seg
        h_f = hf_s[pl.ds(r0, tc), :] + pf_s[pl.ds(r0, tc), :] * cf_ref[pl.ds(sgm, 1), :]
        h_b = hb_s[pl.ds(r0, tc), :] + pb_s[pl.ds(r0, tc), :] * cr_ref[pl.ds(sgm, 1), :]
        o_ref[0, pl.ds(t0, tc), :] = ((h_f + h_b) * gy_ref[0, pl.ds(t0, tc), :].astype(F32)).astype(BF16)
        return 0

    lax.fori_loop(0, n_chunks, combine, 0)


def _lru(xr, gy, conv_w, conv_b, wg, bg, lam_f, lam_b):
    B, S, C = xr.shape
    W = LANES
    n_grp = C // W
    seq = pl.BlockSpec((1, S, W), lambda b, c: (b, 0, c))
    vec = pl.BlockSpec((1, W), lambda b, c: (0, c))
    seg_rows = S + SUBLANES * SUBLANES
    return pl.pallas_call(
        _lru_kernel,
        grid=(B, n_grp),
        in_specs=[
            seq, seq,
            pl.BlockSpec((conv_w.shape[0], W), lambda b, c: (0, c)),
            vec,
            pl.BlockSpec((1, W, 4 * W), lambda b, c: (c, 0, 0)),
            pl.BlockSpec((1, 1, 4 * W), lambda b, c: (c, 0, 0)),
            vec, vec,
        ],
        out_specs=seq,
        out_shape=jax.ShapeDtypeStruct((B, S, C), BF16),
        scratch_shapes=[
            pltpu.VMEM((S + 2 * SUBLANES, W), F32),
            pltpu.VMEM((seg_rows, W), F32), pltpu.VMEM((seg_rows, W), F32),
            pltpu.VMEM((seg_rows, W), F32), pltpu.VMEM((seg_rows, W), F32),
            pltpu.VMEM((seg_rows, W), F32), pltpu.VMEM((seg_rows, W), F32),
            pltpu.VMEM((seg_rows, W), F32), pltpu.VMEM((seg_rows, W), F32),
            pltpu.VMEM((SUBLANES, W), F32), pltpu.VMEM((SUBLANES, W), F32),
        ],
        compiler_params=pltpu.CompilerParams(
            dimension_semantics=("arbitrary", "arbitrary"),
            vmem_limit_bytes=VMEM_LIMIT_BYTES),
        name="rg_lru",
    )(xr, gy, conv_w, conv_b, wg, bg, lam_f, lam_b)


def _out_proj_kernel(ot_ref, lru_ref, x_ref, wa_ref, wl_ref, x1_ref):
    mixed = lax.dot_general(ot_ref[0], wa_ref[...], (((0,), (0,)), ((), ())),
                            preferred_element_type=F32)
    mixed = mixed + jnp.dot(lru_ref[0], wl_ref[...], preferred_element_type=F32)
    x1_ref[0] = x_ref[0] + mixed


def _out_proj(ot, lru, x, wo):
    B, S, D = x.shape
    tm = OUT_TM
    return pl.pallas_call(
        _out_proj_kernel,
        grid=(B, S // tm),
        in_specs=[
            pl.BlockSpec((1, ATTN_WIDTH, tm), lambda b, i: (b, 0, i)),
            pl.BlockSpec((1, tm, lru.shape[2]), lambda b, i: (b, i, 0)),
            pl.BlockSpec((1, tm, D), lambda b, i: (b, i, 0)),
            pl.BlockSpec((ATTN_WIDTH, D), lambda b, i: (0, 0)),
            pl.BlockSpec((lru.shape[2], D), lambda b, i: (1, 0)),
        ],
        out_specs=pl.BlockSpec((1, tm, D), lambda b, i: (b, i, 0)),
        out_shape=jax.ShapeDtypeStruct((B, S, D), F32),
        compiler_params=pltpu.CompilerParams(
            dimension_semantics=("arbitrary", "arbitrary"),
            vmem_limit_bytes=VMEM_LIMIT_BYTES),
        name="out_proj",
    )(ot, lru, x, wo, wo)


def _ffn_kernel(xp_ref, x1_ref, xn_ref, g2_ref, wg_ref, wv_ref, cwg_ref, cwv_ref, cbg_ref, cbv_ref,
                wd_ref, gf_ref, o_ref, xs_ref, acc_ref):
    i = pl.program_id(1)
    tm = x1_ref.shape[1]
    gr = tm // SUBLANES
    n_slab = x1_ref.shape[2] // LANES

    def group(s):
        return jnp.concatenate([xs_ref[k, pl.ds(s, gr, stride=SUBLANES), :] for k in range(n_slab)], axis=1)

    def norm(xs, g):
        ms = jnp.mean(xs * xs, axis=-1, keepdims=True)
        return xs * lax.rsqrt(ms + EPS) * g

    g2 = g2_ref[...]
    for k in range(n_slab):
        xs_ref[k] = x1_ref[0, :, k * LANES:(k + 1) * LANES]
    keep_p = jnp.where(i == 0, 0.0, 1.0)
    keep_n = jnp.where(i == pl.num_programs(1) - 1, 0.0, 1.0)
    halo = jnp.concatenate([norm(xp_ref[0], g2) * keep_p, norm(xn_ref[0], g2) * keep_n], axis=0)
    h = jnp.concatenate([norm(group(s), g2).astype(BF16) for s in range(SUBLANES)]
                        + [halo.astype(BF16)], axis=0)
    n_sub = wg_ref.shape[1] // FFN_SUB

    def up(c):
        cols = slice(c * FFN_SUB, (c + 1) * FFN_SUB)
        return (jnp.dot(h, wg_ref[:, cols], preferred_element_type=F32),
                jnp.dot(h, wv_ref[:, cols], preferred_element_type=F32))

    def conv(u, cw, cb):
        grp = [u[s * gr:(s + 1) * gr] for s in range(SUBLANES)]
        t_prev = u[tm + SUBLANES - 1:tm + SUBLANES]
        t_next = u[tm + SUBLANES:tm + SUBLANES + 1]
        before = jnp.concatenate([t_prev, grp[SUBLANES - 1][:gr - 1]], axis=0)
        after = jnp.concatenate([grp[0][1:], t_next], axis=0)
        outs = []
        for s in range(SUBLANES):
            prv = before if s == 0 else grp[s - 1]
            nxt = after if s == SUBLANES - 1 else grp[s + 1]
            outs.append(cw[0:1] * prv + cw[1:2] * grp[s] + cw[2:3] * nxt + cb)
        return outs

    u_next = up(0)
    for c in range(n_sub):
        ug, uv = u_next
        if c + 1 < n_sub:
            u_next = up(c + 1)
        cols = slice(c * FFN_SUB, (c + 1) * FFN_SUB)
        gate = conv(ug, cwg_ref[:, cols], cbg_ref[:, cols])
        val = conv(uv, cwv_ref[:, cols], cbv_ref[:, cols])
        act = jnp.concatenate([(_gelu_tanh(g) * v).astype(BF16) for g, v in zip(gate, val)], axis=0)
        d = jnp.dot(act, wd_ref[c * FFN_SUB:(c + 1) * FFN_SUB, :], preferred_element_type=F32)
        if c == 0:
            acc_ref[...] = d
        else:
            acc_ref[...] += d

    gf = gf_ref[...]
    for s in range(SUBLANES):
        y = norm(group(s) + acc_ref[s * gr:(s + 1) * gr, :], gf)
        for k in range(n_slab):
            xs_ref[k, pl.ds(s, gr, stride=SUBLANES), :] = y[:, k * LANES:(k + 1) * LANES]
    for k in range(n_slab):
        o_ref[0, :, k * LANES:(k + 1) * LANES] = xs_ref[k]


def _ffn(x1, g2, w_up, cw, cb, w_down, gf):
    B, S, D = x1.shape
    tm = FFN_TM
    d_ff = w_down.shape[0]
    hb = tm // SUBLANES
    n_hb = S // SUBLANES
    resident = dict(pipeline_mode=pl.Buffered(1))
    return pl.pallas_call(
        _ffn_kernel,
        grid=(B, S // tm),
        in_specs=[
            pl.BlockSpec((1, SUBLANES, D), lambda b, i: (b, jnp.maximum(i * hb - 1, 0), 0)),
            pl.BlockSpec((1, tm, D), lambda b, i: (b, i, 0)),
            pl.BlockSpec((1, SUBLANES, D), lambda b, i: (b, jnp.minimum((i + 1) * hb, n_hb - 1), 0)),
            pl.BlockSpec((1, D), lambda b, i: (0, 0)),
            pl.BlockSpec((D, d_ff), lambda b, i: (0, 0), **resident),
            pl.BlockSpec((D, d_ff), lambda b, i: (0, 1), **resident),
            pl.BlockSpec((cw.shape[0], d_ff), lambda b, i: (0, 0)),
            pl.BlockSpec((cw.shape[0], d_ff), lambda b, i: (0, 1)),
            pl.BlockSpec((1, d_ff), lambda b, i: (0, 0)),
            pl.BlockSpec((1, d_ff), lambda b, i: (0, 1)),
            pl.BlockSpec((d_ff, D), lambda b, i: (0, 0), **resident),
            pl.BlockSpec((1, D), lambda b, i: (0, 0)),
        ],
        out_specs=pl.BlockSpec((1, tm, D), lambda b, i: (b, i, 0)),
        out_shape=jax.ShapeDtypeStruct((B, S, D), F32),
        scratch_shapes=[pltpu.VMEM((D // LANES, tm, LANES), F32), pltpu.VMEM((tm, D), F32)],
        compiler_params=pltpu.CompilerParams(
            dimension_semantics=("arbitrary", "arbitrary"),
            vmem_limit_bytes=VMEM_LIMIT_BYTES),
        name="conv_ffn",
    )(x1, x1, x1, g2, w_up, w_up, cw, cw, cb, cb, w_down, gf)


def _rope_tables(S):
    t = jnp.arange(S, dtype=jnp.int32)
    pos = jnp.stack([t // GRID_W, t % GRID_W], axis=1).astype(F32)
    quarter = HEAD_DIM // 4
    inv_freq = 1.0 / (ROPE_THETA ** (jnp.arange(0, 2 * quarter, 2, dtype=F32) / (2 * quarter)))
    ang = pos[:, :, None] * inv_freq[None, None, :]
    cos = jnp.cos(ang)
    sin = jnp.sin(ang)
    zero = jnp.zeros_like(sin)
    cos_h = jnp.concatenate([cos, cos], axis=2).reshape(S, HEAD_DIM)
    sa_h = jnp.concatenate([-sin, zero], axis=2).reshape(S, HEAD_DIM)
    sb_h = jnp.concatenate([zero, sin], axis=2).reshape(S, HEAD_DIM)
    rep = LANES // HEAD_DIM
    return (jnp.tile(cos_h, (1, rep)), jnp.tile(sa_h, (1, rep)), jnp.tile(sb_h, (1, rep)))


def _block_diag_pairs(w):
    nb, bs, _ = w.shape
    w2 = w.reshape(nb // 2, 2, bs, bs)
    z = jnp.zeros((nb // 2, bs, bs), w.dtype)
    top = jnp.concatenate([w2[:, 0], z], axis=2)
    bot = jnp.concatenate([z, w2[:, 1]], axis=2)
    return jnp.concatenate([top, bot], axis=1)


def kernel(x, norm1_g, w_in, q_norm_g, k_norm_g, lru_conv_w, lru_conv_b, wa_f, ba_f, wx_f, bx_f, lam_f,
           wa_b, ba_b, wx_b, bx_b, lam_b, w_out, norm2_g, w_up, up_conv_w, up_conv_b, w_down, final_g):
    B, S, D = x.shape
    depth = w_in.shape[0]
    assert depth == 1, "the final RMSNorm is fused into the (single) layer's channel mixer"
    lru_w = lam_f.shape[1]
    n_grp = lru_w // LANES
    cos, sa, sb = _rope_tables(S)
    idx = jnp.arange(MXU_DIM) // HEAD_DIM
    e = (idx[:, None] == idx[None, :]).astype(BF16)

    for l in range(depth):
        gqk = jnp.concatenate([jnp.tile(q_norm_g[l] * (HEAD_DIM ** -0.5 * math.log2(math.e)), N_HEADS),
                               jnp.tile(k_norm_g[l], N_KV_HEADS)])[None, :]
        q, k, vt, xr, gy = _in_proj(x, norm1_g[l][None, :], w_in[l].astype(BF16), gqk, cos, sa, sb, e)
        c = (ATTN_BOUND_MARGIN * HEAD_DIM) * jnp.max(jnp.abs(gqk[:, :ATTN_WIDTH])) \
            * jnp.max(jnp.abs(gqk[:, ATTN_WIDTH:]))
        ot = _attention(q, k, vt, c.reshape(1))

        wg = (0.5 * jnp.concatenate([_block_diag_pairs(w[l]) for w in (wa_f, wx_f, wa_b, wx_b)],
                                    axis=2)).astype(BF16)
        bg = 0.5 * jnp.concatenate([b[l].reshape(n_grp, 1, LANES) for b in (ba_f, bx_f, ba_b, bx_b)],
                                   axis=2)
        lru = _lru(xr, gy, lru_conv_w[l], lru_conv_b[l][None, :], wg, bg,
                   lam_f[l][None, :], lam_b[l][None, :])

        assert lru_w == ATTN_WIDTH, "w_out is fetched as two equal row blocks"
        x1 = _out_proj(ot, lru, x, w_out[l].astype(BF16))
        x = _ffn(x1, norm2_g[l][None, :], w_up[l].astype(BF16), up_conv_w[l], up_conv_b[l][None, :],
                 w_down[l].astype(BF16), final_g[None, :])
    return x
```
